```python
import jax, jax.numpy as jnp
from jax import lax
import numpy as np

D_MODEL = 1024
BATCH = 8
SEQ = 4096
DEPTH = 1

PLE_DIM = 256
N_ATTN_HEADS = 8
HEAD_DIM = 64
ATTN_WIDTH = N_ATTN_HEADS * HEAD_DIM
SSD_HEADS = 8
SSD_HEAD_DIM = 64
SSD_WIDTH = SSD_HEADS * SSD_HEAD_DIM
SSD_STATE = 128
CONV_WIDTH = 4
CONV_CH = SSD_WIDTH + 2 * SSD_STATE
CHUNK = 128
MIX_WIDTH = ATTN_WIDTH + SSD_WIDTH
IN_PROJ_WIDTH = 3 * ATTN_WIDTH + SSD_WIDTH + CONV_CH + SSD_HEADS
D_FF = 4 * D_MODEL
ROPE_THETA = 10000.0
DILATED_BRANCHES = ((128, 1), (512, 4), (2048, 16))
ATTN_BLOCK = 128
NORM_EPS = 1e-6

kernel_name = 'hybrid_ssd_dilated_attention_layer'


def rms_norm(x, g):
    xf = x.astype(jnp.float32)
    xf = xf * lax.rsqrt(jnp.mean(xf * xf, axis=-1, keepdims=True) + NORM_EPS)
    return xf.astype(x.dtype) * g


def apply_rope(t, positions):
    dh = t.shape[-1]
    half = dh // 2
    inv_freq = ROPE_THETA ** (-jnp.arange(half, dtype=jnp.float32) * 2.0 / dh)
    ang = positions.astype(jnp.float32)[:, :, None] * inv_freq
    cos = jnp.cos(ang)[:, :, None, :]
    sin = jnp.sin(ang)[:, :, None, :]
    tf = t.astype(jnp.float32)
    t1, t2 = tf[..., :half], tf[..., half:]
    return jnp.concatenate([t1 * cos - t2 * sin, t2 * cos + t1 * sin], axis=-1).astype(t.dtype)


def dilated_branch(q, k, v, window, dilation):
    b, s, nh, dh = q.shape
    span = window // dilation
    blk = ATTN_BLOCK
    sub_len = s // dilation
    nb = -(-sub_len // blk)
    sub_pad = nb * blk

    def to_sub(t):
        t = t.reshape(b, sub_len, dilation, nh, dh).transpose(0, 2, 1, 3, 4)
        return jnp.pad(t, ((0, 0), (0, 0), (0, sub_pad - sub_len), (0, 0), (0, 0)))

    def band(t):
        t = jnp.pad(t, ((0, 0), (0, 0), (blk, 0), (0, 0), (0, 0)))
        t = t.reshape(b, dilation, nb + 1, blk, nh, dh)
        return jnp.concatenate([t[:, :, :-1], t[:, :, 1:]], axis=3)

    qb = to_sub(q).reshape(b, dilation, nb, blk, nh, dh)
    kb = band(to_sub(k))
    vb = band(to_sub(v))

    scores = jnp.einsum('brnqhd,brnkhd->brnhqk', qb, kb).astype(jnp.float32)
    qi = jnp.arange(blk)[:, None]
    ki = jnp.arange(2 * blk)[None, :]
    dist = qi + blk - ki
    key_idx = jnp.arange(nb)[:, None, None] * blk - blk + ki
    valid = (dist >= 0) & (dist <= span) & (key_idx >= 0)
    scores = jnp.where(valid[None, None, :, None], scores, -jnp.inf)
    m = jnp.max(scores, axis=-1, keepdims=True)
    e = jnp.exp(scores - m)
    den = jnp.sum(e, axis=-1, keepdims=True)
    out = jnp.einsum('brnhqk,brnkhd->brnqhd', (e / den).astype(v.dtype), vb)
    lse = (m + jnp.log(den))[..., 0]

    out = out.reshape(b, dilation, sub_pad, nh, dh)[:, :, :sub_len]
    out = out.transpose(0, 2, 1, 3, 4).reshape(b, s, nh, dh)
    lse = lse.transpose(0, 1, 2, 4, 3).reshape(b, dilation, sub_pad, nh)[:, :, :sub_len]
    lse = lse.transpose(0, 2, 1, 3).reshape(b, s, nh)
    return out, lse


def causal_conv(u, w, bias):
    y = lax.conv_general_dilated(u, w[:, None, :], window_strides=(1,),
                                 padding=[(CONV_WIDTH - 1, 0)],
                                 dimension_numbers=('NWC', 'WIO', 'NWC'),
                                 feature_group_count=u.shape[-1])
    return y + bias


def segsum_exp(a):
    t = a.shape[-1]
    cs = jnp.cumsum(a, axis=-1)
    diff = cs[..., :, None] - cs[..., None, :]
    mask = jnp.tril(jnp.ones((t, t), dtype=bool))
    return jnp.exp(jnp.where(mask, diff, -jnp.inf))


def ssd_chunked(xdt, adt, bm, cm):
    b, s, nh, hp = xdt.shape
    n = bm.shape[-1]
    c = s // CHUNK
    x_c = xdt.reshape(b, c, CHUNK, nh, hp)
    a_c = adt.reshape(b, c, CHUNK, nh).transpose(0, 3, 1, 2)
    b_c = bm.reshape(b, c, CHUNK, n)
    c_c = cm.reshape(b, c, CHUNK, n)
    a_cs = jnp.cumsum(a_c, axis=-1)

    decay = segsum_exp(a_c)
    cb = jnp.einsum('bcln,bcsn->bcls', c_c, b_c)
    y_diag = jnp.einsum('bcls,bhcls,bcshp->bclhp', cb, decay, x_c)

    decay_states = jnp.exp(a_cs[..., -1:] - a_cs)
    states = jnp.einsum('bcln,bhcl,bclhp->bchpn', b_c, decay_states, x_c)
    chunk_decay = jnp.exp(a_cs[..., -1])

    def step(carry, inp):
        st, dec = inp
        return carry * dec[..., None, None] + st, carry

    init = jnp.zeros_like(states[:, 0])
    _, prev = lax.scan(step, init, (states.transpose(1, 0, 2, 3, 4), chunk_decay.transpose(2, 0, 1)))
    prev = prev.transpose(1, 0, 2, 3, 4)
    y_off = jnp.einsum('bcln,bchpn,bhcl->bclhp', c_c, prev, jnp.exp(a_cs))
    return (y_diag + y_off).reshape(b, s, nh, hp)


def hybrid_mixer(u, positions, w_in, conv_w, conv_b, dt_bias, a_log, d_skip, ssd_norm_g, w_out):
    b, s, _ = u.shape
    proj = u @ w_in
    splits = [ATTN_WIDTH, 2 * ATTN_WIDTH, 3 * ATTN_WIDTH,
              3 * ATTN_WIDTH + SSD_WIDTH, 3 * ATTN_WIDTH + SSD_WIDTH + CONV_CH]
    q, k, v, z, xbc, dt = jnp.split(proj, splits, axis=-1)

    q = apply_rope(q.reshape(b, s, N_ATTN_HEADS, HEAD_DIM), positions) * (HEAD_DIM ** -0.5)
    k = apply_rope(k.reshape(b, s, N_ATTN_HEADS, HEAD_DIM), positions)
    v = v.reshape(b, s, N_ATTN_HEADS, HEAD_DIM)
    outs, lses = [], []
    for window, dilation in DILATED_BRANCHES:
        o, l = dilated_branch(q, k, v, window, dilation)
        outs.append(o)
        lses.append(l)
    alpha = jax.nn.softmax(jnp.stack(lses, axis=0), axis=0)
    attn = jnp.einsum('absh,abshd->bshd', alpha, jnp.stack(outs, axis=0).astype(jnp.float32))
    attn = attn.reshape(b, s, ATTN_WIDTH).astype(u.dtype)

    xbc = jax.nn.silu(causal_conv(xbc, conv_w, conv_b))
    xs, bm, cm = jnp.split(xbc, [SSD_WIDTH, SSD_WIDTH + SSD_STATE], axis=-1)
    xs = xs.reshape(b, s, SSD_HEADS, SSD_HEAD_DIM).astype(jnp.float32)
    dt = jax.nn.softplus(dt.astype(jnp.float32) + dt_bias)
    a = -jnp.exp(a_log.astype(jnp.float32))
    y = ssd_chunked(xs * dt[..., None], dt * a, bm.astype(jnp.float32), cm.astype(jnp.float32))
    y = y + d_skip[:, None] * xs
    y = rms_norm(y.reshape(b, s, SSD_WIDTH) * jax.nn.silu(z.astype(jnp.float32)), ssd_norm_g)
    y = y.astype(u.dtype)

    return jnp.concatenate([attn, y], axis=-1) @ w_out


def setup_inputs(seed: int = 0) -> dict:
    key = jax.random.key(seed)
    ks = jax.random.split(key, 24)
    f32 = jnp.float32

    def gain(k, n):
        return 1.0 + 0.01 * jax.random.normal(k, (DEPTH, n), f32)

    x = jax.random.normal(ks[0], (BATCH, SEQ, D_MODEL), f32)
    p = jax.random.normal(ks[1], (DEPTH, BATCH, SEQ, PLE_DIM), f32)
    offset = jax.random.randint(ks[2], (BATCH, 1), 0, 1024, dtype=jnp.int32)
    positions = (jnp.arange(SEQ, dtype=jnp.int32)[None, :] + offset).astype(jnp.int32)

    w_in = jax.random.normal(ks[3], (DEPTH, D_MODEL, IN_PROJ_WIDTH), f32) * D_MODEL ** -0.5
    conv_w = jax.random.normal(ks[4], (DEPTH, CONV_WIDTH, CONV_CH), f32) * CONV_WIDTH ** -0.5
    conv_b = 0.01 * jax.random.normal(ks[5], (DEPTH, CONV_CH), f32)
    dt0 = jnp.exp(jax.random.uniform(ks[6], (DEPTH, SSD_HEADS), f32, np.log(1e-3), np.log(1e-1)))
    dt_bias = dt0 + jnp.log(-jnp.expm1(-dt0))
    a_log = jnp.log(jax.random.uniform(ks[7], (DEPTH, SSD_HEADS), f32, 1.0, 16.0))
    d_skip = 1.0 + 0.01 * jax.random.normal(ks[8], (DEPTH, SSD_HEADS), f32)
    w_out = jax.random.normal(ks[9], (DEPTH, MIX_WIDTH, D_MODEL), f32) * MIX_WIDTH ** -0.5
    w_up = jax.random.normal(ks[10], (DEPTH, D_MODEL, D_FF), f32) * D_MODEL ** -0.5
    w_down = jax.random.normal(ks[11], (DEPTH, D_FF, D_MODEL), f32) * D_FF ** -0.5
    w_ple_gate = jax.random.normal(ks[12], (DEPTH, D_MODEL, D_MODEL), f32) * D_MODEL ** -0.5
    w_ple_proj = jax.random.normal(ks[13], (DEPTH, PLE_DIM, D_MODEL), f32) * PLE_DIM ** -0.5

    return {
        'x': x, 'p': p, 'positions': positions,
        'norm_mix_pre': gain(ks[14], D_MODEL), 'norm_mix_post': gain(ks[15], D_MODEL),
        'w_in': w_in, 'conv_w': conv_w, 'conv_b': conv_b, 'dt_bias': dt_bias,
        'a_log': a_log, 'd_skip': d_skip, 'ssd_norm_g': gain(ks[16], SSD_WIDTH),
        'w_out': w_out,
        'norm_mlp_pre': gain(ks[17], D_MODEL), 'norm_mlp_post': gain(ks[18], D_MODEL),
        'w_up': w_up, 'w_down': w_down,
        'w_ple_gate': w_ple_gate, 'w_ple_proj': w_ple_proj,
        'norm_ple_post': gain(ks[19], D_MODEL),
    }


def reference(x, p, positions, norm_mix_pre, norm_mix_post, w_in, conv_w, conv_b, dt_bias,
              a_log, d_skip, ssd_norm_g, w_out, norm_mlp_pre, norm_mlp_post, w_up, w_down,
              w_ple_gate, w_ple_proj, norm_ple_post):
    h = x
    for i in range(DEPTH):
        u = rms_norm(h, norm_mix_pre[i])
        mix = hybrid_mixer(u, positions, w_in[i], conv_w[i], conv_b[i], dt_bias[i], a_log[i],
                           d_skip[i], ssd_norm_g[i], w_out[i])
        h = h + rms_norm(mix, norm_mix_post[i])
        u = rms_norm(h, norm_mlp_pre[i])
        ff = jnp.square(jax.nn.relu(u @ w_up[i])) @ w_down[i]
        h = h + rms_norm(ff, norm_mlp_post[i])
        ple = (p[i] @ w_ple_proj[i]) * jax.nn.sigmoid(h @ w_ple_gate[i])
        h = h + rms_norm(ple, norm_ple_post[i])
    return h
```

```python
import functools

import jax
import jax.numpy as jnp
from jax import lax
from jax.experimental import pallas as pl
from jax.experimental.pallas import tpu as pltpu

D_MODEL = 1024
PLE_DIM = 256
N_HEADS = 8
HEAD_DIM = 64
ATTN_WIDTH = N_HEADS * HEAD_DIM
SSD_WIDTH = 512
SSD_STATE = 128
CONV_WIDTH = 4
CONV_CH = SSD_WIDTH + 2 * SSD_STATE
CHUNK = 128
D_FF = 4 * D_MODEL
ROPE_THETA = 10000.0
BRANCHES = ((128, 1), (512, 4), (2048, 16))
ATTN_BLOCK = 128
NORM_EPS = 1e-6

LANES = 128
CONV_HALO = 8
PROJ_TILE = 512
TAIL_TILE = 512
FF_CHUNK = 1024
VMEM_LIMIT = 56 * 1024 * 1024

F32 = jnp.float32
BF16 = jnp.bfloat16


def _dot(a, b):
    return jnp.dot(a, b, preferred_element_type=F32)


def _dot_nt(a, b):
    return lax.dot_general(a, b, (((1,), (1,)), ((), ())), preferred_element_type=F32)


def _sigmoid(x):
    return 1.0 / (1.0 + jnp.exp(-x))


def _rms(x):
    return x * lax.rsqrt(jnp.mean(x * x, axis=-1, keepdims=True) + NORM_EPS)


def _in_proj_kernel(x_ref, pos_ref, invf_ref, g_ref, wq_ref, wk_ref, wv_ref, wz_ref,
                    wxbc_ref, wdt_ref, convw_ref, convb_ref, dtb_ref,
                    q_ref, k_ref, v_ref, z_ref, xs_ref, b_ref, c_ref, dt_ref, cbuf):
    t = x_ref.shape[1]
    j = pl.program_id(1)
    u = (_rms(x_ref[0]) * g_ref[...]).astype(BF16)

    ang = invf_ref[...] * pos_ref[0].astype(F32)
    cos4 = jnp.concatenate([jnp.cos(ang)] * 4, axis=0).T
    sin4 = jnp.concatenate([jnp.sin(ang)] * 4, axis=0).T
    lane = lax.broadcasted_iota(jnp.int32, (t, LANES), 1)
    first_half = (lane % HEAD_DIM) < (HEAD_DIM // 2)
    sin_signed = jnp.where(first_half, -sin4, sin4)

    def rope(v):
        partner = jnp.where(first_half, pltpu.roll(v, LANES - HEAD_DIM // 2, 1),
                            pltpu.roll(v, HEAD_DIM // 2, 1))
        return v * cos4 + partner * sin_signed

    q = _dot(u, wq_ref[...])
    k = _dot(u, wk_ref[...])
    for g in range(ATTN_WIDTH // LANES):
        sl = slice(g * LANES, (g + 1) * LANES)
        q_ref[0, :, sl] = (rope(q[:, sl]) * (HEAD_DIM ** -0.5)).astype(BF16)
        k_ref[0, :, sl] = rope(k[:, sl]).astype(BF16)
    v_ref[0] = _dot(u, wv_ref[...]).astype(BF16)
    z_ref[0] = _dot(u, wz_ref[...]).astype(BF16)

    xbc = _dot(u, wxbc_ref[...])

    @pl.when(j == 0)
    def _():
        cbuf[0:CONV_HALO, :] = jnp.zeros((CONV_HALO, CONV_CH), F32)

    cbuf[CONV_HALO:CONV_HALO + t, :] = xbc
    w = convw_ref[...]
    y = convb_ref[...] + w[CONV_WIDTH - 1:CONV_WIDTH] * xbc
    for s in range(1, CONV_WIDTH):
        y = y + w[CONV_WIDTH - 1 - s:CONV_WIDTH - s] * cbuf[CONV_HALO - s:CONV_HALO - s + t, :]
    cbuf[0:CONV_HALO, :] = cbuf[t:t + CONV_HALO, :]
    act = y * _sigmoid(y)
    xs_ref[0] = act[:, :SSD_WIDTH].astype(BF16)
    b_ref[0] = act[:, SSD_WIDTH:SSD_WIDTH + SSD_STATE].astype(BF16)
    c_ref[0] = act[:, SSD_WIDTH + SSD_STATE:].astype(BF16)

    dtr = _dot(u, wdt_ref[...]) + dtb_ref[...]
    dt_ref[0] = jnp.maximum(dtr, 0.0) + jnp.log1p(jnp.exp(-jnp.abs(dtr)))


def _in_proj(x, pos3, invf, g, wq, wk, wv, wz, wxbc, wdt, convw, convb, dtb):
    b, s, _ = x.shape
    t = PROJ_TILE
    tile = lambda w_: pl.BlockSpec((1, t, w_), lambda bi, j: (bi, j, 0))
    full = lambda a: pl.BlockSpec(a.shape, lambda bi, j: (0,) * a.ndim)
    wide = jax.ShapeDtypeStruct((b, s, ATTN_WIDTH), BF16)
    narrow = jax.ShapeDtypeStruct((b, s, SSD_STATE), BF16)
    return pl.pallas_call(
        _in_proj_kernel,
        grid=(b, s // t),
        in_specs=[tile(D_MODEL), pl.BlockSpec((1, 1, t), lambda bi, j: (bi, 0, j)),
                  full(invf), full(g), full(wq), full(wk), full(wv), full(wz), full(wxbc),
                  full(wdt), full(convw), full(convb), full(dtb)],
        out_specs=[tile(ATTN_WIDTH)] * 5 + [tile(SSD_STATE)] * 3,
        out_shape=[wide] * 5 + [narrow, narrow, jax.ShapeDtypeStruct((b, s, LANES), F32)],
        scratch_shapes=[pltpu.VMEM((t + CONV_HALO, CONV_CH), F32)],
        compiler_params=pltpu.CompilerParams(
            dimension_semantics=("arbitrary", "arbitrary"), vmem_limit_bytes=VMEM_LIMIT),
        name="in_proj",
    )(x, pos3, invf, g, wq, wk, wv, wz, wxbc, wdt, convw, convb, dtb)


def _expand_heads(m):
    rows = m.shape[0]
    lane = lax.broadcasted_iota(jnp.int32, (rows, LANES), 1)
    parts = []
    for g in range(SSD_WIDTH // LANES):
        lo = jnp.broadcast_to(m[:, 2 * g:2 * g + 1], (rows, LANES))
        hi = jnp.broadcast_to(m[:, 2 * g + 1:2 * g + 2], (rows, LANES))
        parts.append(jnp.where(lane < HEAD_DIM, lo, hi))
    return jnp.concatenate(parts, axis=1)


def _ssd_kernel(xs_ref, b_ref, c_ref, dt_ref, z_ref, alog_ref, dskip_ref, g_ref, y_ref, state):
    @pl.when(pl.program_id(1) == 0)
    def _():
        state[...] = jnp.zeros(state.shape, F32)

    row = lax.broadcasted_iota(jnp.int32, (CHUNK, CHUNK), 0)
    col = lax.broadcasted_iota(jnp.int32, (CHUNK, CHUNK), 1)
    causal = row >= col

    dt = dt_ref[0]
    adt = dt * (-jnp.exp(alog_ref[...]))
    cs = jnp.dot(causal.astype(F32), adt, preferred_element_type=F32,
                 precision=lax.Precision.HIGHEST)
    cs_t = cs.T
    xs = xs_ref[0].astype(F32)
    xdt = xs * _expand_heads(dt)
    xdt_b = xdt.astype(BF16)
    bm = b_ref[0]
    cm = c_ref[0]
    cb = _dot_nt(cm, bm)

    diag = []
    for g in range(SSD_WIDTH // LANES):
        xg = xdt_b[:, g * LANES:(g + 1) * LANES]
        halves = []
        for h in (2 * g, 2 * g + 1):
            seg = jnp.where(causal, cs[:, h:h + 1] - cs_t[h:h + 1, :], -jnp.inf)
            halves.append(_dot((cb * jnp.exp(seg)).astype(BF16), xg))
        diag.append(jnp.where(col < HEAD_DIM, halves[0], halves[1]))
    y = jnp.concatenate(diag, axis=1)

    prev = state[...]
    grow = _expand_heads(jnp.exp(cs))
    y = y + _dot(cm, prev.astype(BF16)) * grow
    to_end = _expand_heads(jnp.exp(cs[CHUNK - 1:CHUNK, :] - cs))
    inject = _dot(bm.astype(F32).T.astype(BF16), (xdt * to_end).astype(BF16))
    state[...] = prev * grow[CHUNK - 1:CHUNK, :] + inject

    y = y + dskip_ref[...] * xs
    zf = z_ref[0].astype(F32)
    y_ref[0] = (_rms(y * (zf * _sigmoid(zf))) * g_ref[...]).astype(BF16)


def _ssd(xs, bm, cm, dt, z, alog, dskip, g):
    b, s, _ = xs.shape
    blk = lambda w_: pl.BlockSpec((1, CHUNK, w_), lambda bi, c: (bi, c, 0))
    full = lambda a: pl.BlockSpec(a.shape, lambda bi, c: (0,) * a.ndim)
    return pl.pallas_call(
        _ssd_kernel,
        grid=(b, s // CHUNK),
        in_specs=[blk(SSD_WIDTH), blk(SSD_STATE), blk(SSD_STATE), blk(LANES), blk(SSD_WIDTH),
                  full(alog), full(dskip), full(g)],
        out_specs=blk(SSD_WIDTH),
        out_shape=jax.ShapeDtypeStruct((b, s, SSD_WIDTH), BF16),
        scratch_shapes=[pltpu.VMEM((SSD_STATE, SSD_WIDTH), F32)],
        compiler_params=pltpu.CompilerParams(
            dimension_semantics=("arbitrary", "arbitrary"), vmem_limit_bytes=VMEM_LIMIT),
        name="ssd",
    )(xs, bm, cm, dt, z, alog, dskip, g)


def _attn_block(has_prev, span, q_ref, kp_ref, kc_ref, vp_ref, vc_ref, o_ref, l_ref):
    blk = ATTN_BLOCK
    qi = lax.broadcasted_iota(jnp.int32, (blk, 2 * blk), 0)
    ki = lax.broadcasted_iota(jnp.int32, (blk, 2 * blk), 1)
    dist = qi + blk - ki
    valid = (dist >= 0) & (dist <= span) & ((ki >= blk) | has_prev)
    lane = lax.broadcasted_iota(jnp.int32, (blk, LANES), 1)
    low = lane < HEAD_DIM

    q = q_ref[0]
    k = jnp.concatenate([kp_ref[0], kc_ref[0]], axis=0)
    v = jnp.concatenate([vp_ref[0], vc_ref[0]], axis=0)
    lse_all = jnp.zeros((blk, LANES), F32)
    for g in range(ATTN_WIDTH // LANES):
        sl = slice(g * LANES, (g + 1) * LANES)
        q2, k2, v2 = q[:, sl], k[:, sl], v[:, sl]
        outs = []
        for e in range(2):
            qm = jnp.where(low if e == 0 else ~low, q2, jnp.zeros_like(q2))
            s = jnp.where(valid, _dot_nt(qm, k2), -jnp.inf)
            m = jnp.max(s, axis=-1, keepdims=True)
            p = jnp.exp(s - m)
            den = jnp.sum(p, axis=-1, keepdims=True)
            outs.append(_dot((p / den).astype(BF16), v2))
            lse_all = jnp.where(lane == 2 * g + e, m + jnp.log(den), lse_all)
        o_ref[0, :, sl] = jnp.where(low, outs[0], outs[1]).astype(BF16)
    l_ref[0] = lse_all


def _attn_kernel(steps, *refs):
    n_in = 5 * len(BRANCHES)
    i = pl.program_id(1)
    for bi, (window, dilation) in enumerate(BRANCHES):
        nb = steps // dilation
        _attn_block((i % nb) > 0, window // dilation, *refs[5 * bi:5 * bi + 5],
                    *refs[n_in + 2 * bi:n_in + 2 * bi + 2])


def _attention(q, k, v):
    b, s, _ = q.shape
    steps = s // ATTN_BLOCK
    args, in_specs, out_specs, out_shape = [], [], [], []
    for _, d in BRANCHES:
        nb = steps // d
        cur = lambda bi, i, nb=nb: (bi, i % nb, i // nb)
        prev = lambda bi, i, nb=nb: (bi, jnp.maximum(i % nb - 1, 0), i // nb)
        view = lambda a, d=d: a.reshape(b, s // d, d * a.shape[-1])
        wide = lambda m: pl.BlockSpec((1, ATTN_BLOCK, ATTN_WIDTH), m)
        args += [view(q), view(k), view(k), view(v), view(v)]
        in_specs += [wide(cur), wide(prev), wide(cur), wide(prev), wide(cur)]
        out_specs += [wide(cur), pl.BlockSpec((1, ATTN_BLOCK, LANES), cur)]
        out_shape += [jax.ShapeDtypeStruct((b, s // d, d * ATTN_WIDTH), BF16),
                      jax.ShapeDtypeStruct((b, s // d, d * LANES), F32)]
    res = pl.pallas_call(
        functools.partial(_attn_kernel, steps),
        grid=(b, steps),
        in_specs=in_specs, out_specs=out_specs, out_shape=out_shape,
        compiler_params=pltpu.CompilerParams(
            dimension_semantics=("arbitrary", "arbitrary"), vmem_limit_bytes=VMEM_LIMIT),
        name="dilated_attention",
    )(*args)
    outs = [r.reshape(b * s, ATTN_WIDTH) for r in res[0::2]]
    lses = [r.reshape(b * s, LANES) for r in res[1::2]]
    return outs, lses


def _tail_kernel(x_ref, o1_ref, l1_ref, o2_ref, l2_ref, o3_ref, l3_ref, y_ref, p_ref, e_ref,
                 gmix_ref, wout_ref, gpre_ref, wup_ref, wdown_ref, gmlp_ref, wgate_ref,
                 wproj_ref, gple_ref, out_ref):
    ls = [l1_ref[...], l2_ref[...], l3_ref[...]]
    lmax = jnp.maximum(jnp.maximum(ls[0], ls[1]), ls[2])
    es = [jnp.exp(l - lmax) for l in ls]
    inv = 1.0 / (es[0] + es[1] + es[2])
    expand = e_ref[...]

    def widen(a):
        hi = a.astype(BF16)
        lo = (a - hi.astype(F32)).astype(BF16)
        return _dot(hi, expand) + _dot(lo, expand)

    attn = (widen(es[0] * inv) * o1_ref[...].astype(F32)
            + widen(es[1] * inv) * o2_ref[...].astype(F32)
            + widen(es[2] * inv) * o3_ref[...].astype(F32))
    mix = (_dot(attn.astype(BF16), wout_ref[0:ATTN_WIDTH, :])
           + _dot(y_ref[...], wout_ref[ATTN_WIDTH:, :]))
    h = x_ref[...] + _rms(mix) * gmix_ref[...]

    u = (_rms(h) * gpre_ref[...]).astype(BF16)
    ff = jnp.zeros(h.shape, F32)
    for c in range(D_FF // FF_CHUNK):
        sl = slice(c * FF_CHUNK, (c + 1) * FF_CHUNK)
        hid = jnp.square(jnp.maximum(_dot(u, wup_ref[:, sl]), 0.0)).astype(BF16)
        ff = ff + _dot(hid, wdown_ref[sl, :])
    h = h + _rms(ff) * gmlp_ref[...]

    gate = _sigmoid(_dot(h.astype(BF16), wgate_ref[...]))
    ple = _dot(p_ref[...].astype(BF16), wproj_ref[...]) * gate
    out_ref[...] = h + _rms(ple) * gple_ref[...]


def _tail(x2, outs, lses, y2, p2, expand, gmix, wout, gpre, wup, wdown, gmlp, wgate, wproj, gple):
    n = x2.shape[0]
    t = TAIL_TILE
    tile = lambda w_: pl.BlockSpec((t, w_), lambda i: (i, 0))
    full = lambda a: pl.BlockSpec(a.shape, lambda i: (0, 0), pipeline_mode=pl.Buffered(1))
    branch = []
    for o, l in zip(outs, lses):
        branch += [o, l]
    return pl.pallas_call(
        _tail_kernel,
        grid=(n // t,),
        in_specs=[tile(D_MODEL)] + [tile(ATTN_WIDTH), tile(LANES)] * 3
                 + [tile(SSD_WIDTH), tile(PLE_DIM), full(expand), full(gmix), full(wout),
                    full(gpre), full(wup), full(wdown), full(gmlp), full(wgate), full(wproj),
                    full(gple)],
        out_specs=tile(D_MODEL),
        out_shape=jax.ShapeDtypeStruct((n, D_MODEL), F32),
        compiler_params=pltpu.CompilerParams(
            dimension_semantics=("arbitrary",), vmem_limit_bytes=VMEM_LIMIT),
        name="tail",
    )(x2, *branch, y2, p2, expand, gmix, wout, gpre, wup, wdown, gmlp, wgate, wproj, gple)


def _pad_lanes(a):
    return jnp.pad(a, ((0, 0), (0, LANES - a.shape[-1])))


def kernel(x, p, positions, norm_mix_pre, norm_mix_post, w_in, conv_w, conv_b, dt_bias, a_log,
           d_skip, ssd_norm_g, w_out, norm_mlp_pre, norm_mlp_post, w_up, w_down, w_ple_gate,
           w_ple_proj, norm_ple_post):
    b, s, _ = x.shape
    half = HEAD_DIM // 2
    invf = (ROPE_THETA ** (-jnp.arange(half, dtype=F32) * 2.0 / HEAD_DIM)).reshape(half, 1)
    pos3 = positions.reshape(b, 1, s)
    expand = (jnp.arange(LANES)[:, None] == jnp.arange(ATTN_WIDTH)[None, :] // HEAD_DIM).astype(BF16)
    row = lambda a: a.reshape(1, -1)

    h = x
    for i in range(w_in.shape[0]):
        wi = w_in[i].astype(BF16)
        c0 = 3 * ATTN_WIDTH
        c1 = c0 + SSD_WIDTH
        c2 = c1 + CONV_CH
        q, k, v, z, xs, bm, cm, dt = _in_proj(
            h, pos3, invf, row(norm_mix_pre[i]),
            wi[:, 0:ATTN_WIDTH], wi[:, ATTN_WIDTH:2 * ATTN_WIDTH], wi[:, 2 * ATTN_WIDTH:c0],
            wi[:, c0:c1], wi[:, c1:c2], _pad_lanes(wi[:, c2:]),
            conv_w[i], row(conv_b[i]), _pad_lanes(row(dt_bias[i])))
        y = _ssd(xs, bm, cm, dt, z, _pad_lanes(row(a_log[i])),
                 row(jnp.repeat(d_skip[i], HEAD_DIM)), row(ssd_norm_g[i]))
        outs, lses = _attention(q, k, v)
        h = _tail(h.reshape(b * s, D_MODEL), outs, lses, y.reshape(b * s, SSD_WIDTH),
                  p[i].reshape(b * s, PLE_DIM), expand, row(norm_mix_post[i]),
                  w_out[i].astype(BF16), row(norm_mlp_pre[i]), w_up[i].astype(BF16),
                  w_down[i].astype(BF16), row(norm_mlp_post[i]), w_ple_gate[i].astype(BF16),
                  w_ple_proj[i].astype(BF16), row(norm_ple_post[i])).reshape(b, s, D_MODEL)
    return h
```

```python
import functools

import jax
import jax.numpy as jnp
from jax import lax
from jax.experimental import pallas as pl
from jax.experimental.pallas import tpu as pltpu

D_MODEL = 1024
PLE_DIM = 256
N_HEADS = 8
HEAD_DIM = 64
ATTN_WIDTH = N_HEADS * HEAD_DIM
SSD_WIDTH = 512
SSD_STATE = 128
CONV_WIDTH = 4
CONV_CH = SSD_WIDTH + 2 * SSD_STATE
CHUNK = 128
D_FF = 4 * D_MODEL
ROPE_THETA = 10000.0
BRANCHES = ((128, 1), (512, 4), (2048, 16))
DILATIONS = tuple(d for _, d in BRANCHES)
ATTN_BLOCK = 128
NORM_EPS = 1e-6

LANES = 128
CONV_HALO = 8
PROJ_TILE = 512
TAIL_TILE = 512
FF_CHUNK = 1024
MERGE_ROWS = 256
ATTN_GROUP = ATTN_BLOCK * max(DILATIONS)
VMEM_LIMIT = 56 * 1024 * 1024

F32 = jnp.float32
BF16 = jnp.bfloat16


def _dot(a, b):
    return jnp.dot(a, b, preferred_element_type=F32)


def _dot_nt(a, b):
    return lax.dot_general(a, b, (((1,), (1,)), ((), ())), preferred_element_type=F32)


def _sigmoid(x):
    return 1.0 / (1.0 + jnp.exp(-x))


def _rms(x):
    return x * lax.rsqrt(jnp.mean(x * x, axis=-1, keepdims=True) + NORM_EPS)


def _in_proj_kernel(x_ref, pos_ref, invf_ref, g_ref, wq_ref, wk_ref, wv_ref, wz_ref,
                    wxbc_ref, wdt_ref, convw_ref, convb_ref, dtb_ref, *rest):
    n_qkv = 3 * len(DILATIONS)
    qkv_refs = rest[:n_qkv]
    z_ref, xs_ref, b_ref, c_ref, dt_ref, cbuf, qs, ks, vs = rest[n_qkv:]
    t = x_ref.shape[1]
    j = pl.program_id(1)
    u = (_rms(x_ref[0]) * g_ref[...]).astype(BF16)

    ang = invf_ref[...] * pos_ref[0].astype(F32)
    cos4 = jnp.concatenate([jnp.cos(ang)] * 4, axis=0).T
    sin4 = jnp.concatenate([jnp.sin(ang)] * 4, axis=0).T
    lane = lax.broadcasted_iota(jnp.int32, (t, LANES), 1)
    first_half = (lane % HEAD_DIM) < (HEAD_DIM // 2)
    sin_signed = jnp.where(first_half, -sin4, sin4)

    def rope(v):
        partner = jnp.where(first_half, pltpu.roll(v, LANES - HEAD_DIM // 2, 1),
                            pltpu.roll(v, HEAD_DIM // 2, 1))
        return v * cos4 + partner * sin_signed

    q = _dot(u, wq_ref[...])
    k = _dot(u, wk_ref[...])
    v = _dot(u, wv_ref[...])
    for g in range(ATTN_WIDTH // LANES):
        sl = slice(g * LANES, (g + 1) * LANES)
        qs[g] = rope(q[:, sl]) * (HEAD_DIM ** -0.5)
        ks[g] = rope(k[:, sl])
        vs[g] = v[:, sl]

    for bi, d in enumerate(DILATIONS):
        for src, dst in zip((qs, ks, vs), qkv_refs[3 * bi:3 * bi + 3]):
            for r in range(d):
                for g in range(ATTN_WIDTH // LANES):
                    c = r * ATTN_WIDTH + g * LANES
                    dst[0, :, c:c + LANES] = src[g, pl.ds(r, t // d, stride=d), :].astype(BF16)

    z_ref[0] = _dot(u, wz_ref[...]).astype(BF16)

    xbc = _dot(u, wxbc_ref[...])

    @pl.when(j == 0)
    def _():
        cbuf[0:CONV_HALO, :] = jnp.zeros((CONV_HALO, CONV_CH), F32)

    cbuf[CONV_HALO:CONV_HALO + t, :] = xbc
    w = convw_ref[...]
    y = convb_ref[...] + w[CONV_WIDTH - 1:CONV_WIDTH] * xbc
    for s in range(1, CONV_WIDTH):
        y = y + w[CONV_WIDTH - 1 - s:CONV_WIDTH - s] * cbuf[CONV_HALO - s:CONV_HALO - s + t, :]
    cbuf[0:CONV_HALO, :] = cbuf[t:t + CONV_HALO, :]
    act = y * _sigmoid(y)
    xs_ref[0] = act[:, :SSD_WIDTH].astype(BF16)
    b_ref[0] = act[:, SSD_WIDTH:SSD_WIDTH + SSD_STATE].astype(BF16)
    c_ref[0] = act[:, SSD_WIDTH + SSD_STATE:].astype(BF16)

    dtr = _dot(u, wdt_ref[...]) + dtb_ref[...]
    dt_ref[0] = jnp.maximum(dtr, 0.0) + jnp.log1p(jnp.exp(-jnp.abs(dtr)))


def _in_proj(x, pos3, invf, g, wq, wk, wv, wz, wxbc, wdt, convw, convb, dtb):
    b, s, _ = x.shape
    t = PROJ_TILE
    tile = lambda w_: pl.BlockSpec((1, t, w_), lambda bi, j: (bi, j, 0))
    full = lambda a: pl.BlockSpec(a.shape, lambda bi, j: (0,) * a.ndim)
    wide = jax.ShapeDtypeStruct((b, s, ATTN_WIDTH), BF16)
    narrow = jax.ShapeDtypeStruct((b, s, SSD_STATE), BF16)
    qkv_specs, qkv_shapes = [], []
    for d in DILATIONS:
        qkv_specs += [pl.BlockSpec((1, t // d, d * ATTN_WIDTH), lambda bi, j: (bi, j, 0))] * 3
        qkv_shapes += [jax.ShapeDtypeStruct((b, s // d, d * ATTN_WIDTH), BF16)] * 3
    return pl.pallas_call(
        _in_proj_kernel,
        grid=(b, s // t),
        in_specs=[tile(D_MODEL), pl.BlockSpec((1, 1, t), lambda bi, j: (bi, 0, j)),
                  full(invf), full(g), full(wq), full(wk), full(wv), full(wz), full(wxbc),
                  full(wdt), full(convw), full(convb), full(dtb)],
        out_specs=qkv_specs + [tile(ATTN_WIDTH)] * 2 + [tile(SSD_STATE)] * 3,
        out_shape=qkv_shapes + [wide] * 2
                  + [narrow, narrow, jax.ShapeDtypeStruct((b, s, LANES), F32)],
        scratch_shapes=[pltpu.VMEM((t + CONV_HALO, CONV_CH), F32)]
                       + [pltpu.VMEM((ATTN_WIDTH // LANES, t, LANES), F32)] * 3,
        compiler_params=pltpu.CompilerParams(
            dimension_semantics=("arbitrary", "arbitrary"), vmem_limit_bytes=VMEM_LIMIT),
        name="in_proj",
    )(x, pos3, invf, g, wq, wk, wv, wz, wxbc, wdt, convw, convb, dtb)


def _expand_heads(m):
    rows = m.shape[0]
    lane = lax.broadcasted_iota(jnp.int32, (rows, LANES), 1)
    parts = []
    for g in range(SSD_WIDTH // LANES):
        lo = jnp.broadcast_to(m[:, 2 * g:2 * g + 1], (rows, LANES))
        hi = jnp.broadcast_to(m[:, 2 * g + 1:2 * g + 2], (rows, LANES))
        parts.append(jnp.where(lane < HEAD_DIM, lo, hi))
    return jnp.concatenate(parts, axis=1)


def _ssd_kernel(xs_ref, b_ref, c_ref, dt_ref, z_ref, alog_ref, dskip_ref, g_ref, y_ref, state):
    @pl.when(pl.program_id(1) == 0)
    def _():
        state[...] = jnp.zeros(state.shape, F32)

    row = lax.broadcasted_iota(jnp.int32, (CHUNK, CHUNK), 0)
    col = lax.broadcasted_iota(jnp.int32, (CHUNK, CHUNK), 1)
    causal = row >= col

    dt = dt_ref[0]
    adt = dt * (-jnp.exp(alog_ref[...]))
    cs = jnp.dot(causal.astype(F32), adt, preferred_element_type=F32,
                 precision=lax.Precision.HIGHEST)
    cs_t = cs.T
    xs = xs_ref[0].astype(F32)
    xdt = xs * _expand_heads(dt)
    xdt_b = xdt.astype(BF16)
    bm = b_ref[0]
    cm = c_ref[0]
    cb = _dot_nt(cm, bm)

    diag = []
    for g in range(SSD_WIDTH // LANES):
        xg = xdt_b[:, g * LANES:(g + 1) * LANES]
        halves = []
        for h in (2 * g, 2 * g + 1):
            seg = jnp.where(causal, cs[:, h:h + 1] - cs_t[h:h + 1, :], -jnp.inf)
            halves.append(_dot((cb * jnp.exp(seg)).astype(BF16), xg))
        diag.append(jnp.where(col < HEAD_DIM, halves[0], halves[1]))
    y = jnp.concatenate(diag, axis=1)

    prev = state[...]
    grow = _expand_heads(jnp.exp(cs))
    y = y + _dot(cm, prev.astype(BF16)) * grow
    to_end = _expand_heads(jnp.exp(cs[CHUNK - 1:CHUNK, :] - cs))
    inject = _dot(bm.astype(F32).T.astype(BF16), (xdt * to_end).astype(BF16))
    state[...] = prev * grow[CHUNK - 1:CHUNK, :] + inject

    y = y + dskip_ref[...] * xs
    zf = z_ref[0].astype(F32)
    y_ref[0] = (_rms(y * (zf * _sigmoid(zf))) * g_ref[...]).astype(BF16)


def _ssd(xs, bm, cm, dt, z, alog, dskip, g):
    b, s, _ = xs.shape
    blk = lambda w_: pl.BlockSpec((1, CHUNK, w_), lambda bi, c: (bi, c, 0))
    full = lambda a: pl.BlockSpec(a.shape, lambda bi, c: (0,) * a.ndim)
    return pl.pallas_call(
        _ssd_kernel,
        grid=(b, s // CHUNK),
        in_specs=[blk(SSD_WIDTH), blk(SSD_STATE), blk(SSD_STATE), blk(LANES), blk(SSD_WIDTH),
                  full(alog), full(dskip), full(g)],
        out_specs=blk(SSD_WIDTH),
        out_shape=jax.ShapeDtypeStruct((b, s, SSD_WIDTH), BF16),
        scratch_shapes=[pltpu.VMEM((SSD_STATE, SSD_WIDTH), F32)],
        compiler_params=pltpu.CompilerParams(
            dimension_semantics=("arbitrary", "arbitrary"), vmem_limit_bytes=VMEM_LIMIT),
        name="ssd",
    )(xs, bm, cm, dt, z, alog, dskip, g)


def _attn_block(has_prev, span, q_ref, kp_ref, kc_ref, vp_ref, vc_ref):
    blk = ATTN_BLOCK
    qi = lax.broadcasted_iota(jnp.int32, (blk, 2 * blk), 0)
    ki = lax.broadcasted_iota(jnp.int32, (blk, 2 * blk), 1)
    dist = qi + blk - ki
    valid = (dist >= 0) & (dist <= span) & ((ki >= blk) | has_prev)
    lane = lax.broadcasted_iota(jnp.int32, (blk, LANES), 1)
    low = lane < HEAD_DIM

    q = q_ref[0]
    k = jnp.concatenate([kp_ref[0], kc_ref[0]], axis=0)
    v = jnp.concatenate([vp_ref[0], vc_ref[0]], axis=0)
    lse_all = jnp.zeros((blk, LANES), F32)
    groups = []
    for g in range(ATTN_WIDTH // LANES):
        sl = slice(g * LANES, (g + 1) * LANES)
        q2, k2, v2 = q[:, sl], k[:, sl], v[:, sl]
        outs = []
        for e in range(2):
            qm = jnp.where(low if e == 0 else ~low, q2, jnp.zeros_like(q2))
            s = jnp.where(valid, _dot_nt(qm, k2), -jnp.inf)
            m = jnp.max(s, axis=-1, keepdims=True)
            p = jnp.exp(s - m)
            den = jnp.sum(p, axis=-1, keepdims=True)
            outs.append(_dot((p / den).astype(BF16), v2))
            lse_all = jnp.where(lane == 2 * g + e, m + jnp.log(den), lse_all)
        groups.append(jnp.where(low, outs[0], outs[1]))
    return groups, lse_all


def _attn_kernel(steps, *refs):
    nbr = len(BRANCHES)
    in_refs = refs[:5 * nbr]
    e_ref, out_ref = refs[5 * nbr:5 * nbr + 2]
    acc_o = refs[5 * nbr + 2:5 * nbr + 2 + nbr]
    acc_l = refs[5 * nbr + 2 + nbr:]
    i = pl.program_id(1)
    group_steps = ATTN_GROUP // ATTN_BLOCK
    local = i % group_steps

    for bi, (window, d) in enumerate(BRANCHES):
        r = i % d
        n = i // d
        o, lse = _attn_block(n > 0, window // d, *in_refs[5 * bi:5 * bi + 5])
        start = (local // d) * (d * ATTN_BLOCK) + r
        rows = pl.ds(start, ATTN_BLOCK) if d == 1 else pl.ds(start, ATTN_BLOCK, stride=d)
        for g, og in enumerate(o):
            acc_o[bi][g, rows, :] = og
        acc_l[bi][rows, :] = lse

    @pl.when(local == group_steps - 1)
    def _():
        expand = e_ref[...]

        def widen(a):
            hi = a.astype(BF16)
            lo = (a - hi.astype(F32)).astype(BF16)
            return _dot(hi, expand) + _dot(lo, expand)

        def merge(c, carry):
            rows = pl.ds(pl.multiple_of(c * MERGE_ROWS, MERGE_ROWS), MERGE_ROWS)
            ls = [a[rows, :] for a in acc_l]
            lmax = functools.reduce(jnp.maximum, ls)
            es = [jnp.exp(l - lmax) for l in ls]
            inv = 1.0 / functools.reduce(lambda a, b_: a + b_, es)
            ws = [widen(e * inv) for e in es]
            for g in range(ATTN_WIDTH // LANES):
                sl = slice(g * LANES, (g + 1) * LANES)
                attn = functools.reduce(
                    lambda a, b_: a + b_, [w[:, sl] * a[g, rows, :] for w, a in zip(ws, acc_o)])
                out_ref[0, rows, sl] = attn.astype(BF16)
            return carry

        lax.fori_loop(0, ATTN_GROUP // MERGE_ROWS, merge, 0)


def _attention(qkv, expand):
    b = qkv[0].shape[0]
    s = qkv[0].shape[1] * DILATIONS[0]
    steps = s // ATTN_BLOCK
    group_steps = ATTN_GROUP // ATTN_BLOCK
    in_specs = []
    for d in DILATIONS:
        cur = lambda bi, i, d=d: (bi, i // d, i % d)
        prev = lambda bi, i, d=d: (bi, jnp.maximum(i // d - 1, 0), i % d)
        wide = lambda m: pl.BlockSpec((1, ATTN_BLOCK, ATTN_WIDTH), m)
        in_specs += [wide(cur), wide(prev), wide(cur), wide(prev), wide(cur)]
    args = []
    for bi in range(len(DILATIONS)):
        q, k, v = qkv[3 * bi:3 * bi + 3]
        args += [q, k, k, v, v]
    nbr = len(BRANCHES)
    return pl.pallas_call(
        functools.partial(_attn_kernel, steps),
        grid=(b, steps),
        in_specs=in_specs + [pl.BlockSpec(expand.shape, lambda bi, i: (0, 0))],
        out_specs=pl.BlockSpec((1, ATTN_GROUP, ATTN_WIDTH), lambda bi, i: (bi, i // group_steps, 0)),
        out_shape=jax.ShapeDtypeStruct((b, s, ATTN_WIDTH), BF16),
        scratch_shapes=[pltpu.VMEM((ATTN_WIDTH // LANES, ATTN_GROUP, LANES), F32)] * nbr
                       + [pltpu.VMEM((ATTN_GROUP, LANES), F32)] * nbr,
        compiler_params=pltpu.CompilerParams(
            dimension_semantics=("arbitrary", "arbitrary"), vmem_limit_bytes=VMEM_LIMIT),
        name="dilated_attention",
    )(*args, expand)


def _tail_kernel(x_ref, attn_ref, y_ref, p_ref, gmix_ref, wout_ref, gpre_ref, wup_ref,
                 wdown_ref, gmlp_ref, wgate_ref, wproj_ref, gple_ref, out_ref):
    mix = (_dot(attn_ref[...], wout_ref[0:ATTN_WIDTH, :])
           + _dot(y_ref[...], wout_ref[ATTN_WIDTH:, :]))
    h = x_ref[...] + _rms(mix) * gmix_ref[...]

    u = (_rms(h) * gpre_ref[...]).astype(BF16)
    ff = jnp.zeros(h.shape, F32)
    for c in range(D_FF // FF_CHUNK):
        sl = slice(c * FF_CHUNK, (c + 1) * FF_CHUNK)
        hid = jnp.square(jnp.maximum(_dot(u, wup_ref[:, sl]), 0.0)).astype(BF16)
        ff = ff + _dot(hid, wdown_ref[sl, :])
    h = h + _rms(ff) * gmlp_ref[...]

    gate = _sigmoid(_dot(h.astype(BF16), wgate_ref[...]))
    ple = _dot(p_ref[...].astype(BF16), wproj_ref[...]) * gate
    out_ref[...] = h + _rms(ple) * gple_ref[...]


def _tail(x2, attn2, y2, p2, gmix, wout, gpre, wup, wdown, gmlp, wgate, wproj, gple):
    n = x2.shape[0]
    t = TAIL_TILE
    tile = lambda w_: pl.BlockSpec((t, w_), lambda i: (i, 0))
    full = lambda a: pl.BlockSpec(a.shape, lambda i: (0, 0), pipeline_mode=pl.Buffered(1))
    return pl.pallas_call(
        _tail_kernel,
        grid=(n // t,),
        in_specs=[tile(D_MODEL), tile(ATTN_WIDTH), tile(SSD_WIDTH), tile(PLE_DIM), full(gmix),
                  full(wout), full(gpre), full(wup), full(wdown), full(gmlp), full(wgate),
                  full(wproj), full(gple)],
        out_specs=tile(D_MODEL),
        out_shape=jax.ShapeDtypeStruct((n, D_MODEL), F32),
        compiler_params=pltpu.CompilerParams(
            dimension_semantics=("arbitrary",), vmem_limit_bytes=VMEM_LIMIT),
        name="tail",
    )(x2, attn2, y2, p2, gmix, wout, gpre, wup, wdown, gmlp, wgate, wproj, gple)


def _pad_lanes(a):
    return jnp.pad(a, ((0, 0), (0, LANES - a.shape[-1])))


def kernel(x, p, positions, norm_mix_pre, norm_mix_post, w_in, conv_w, conv_b, dt_bias, a_log,
           d_skip, ssd_norm_g, w_out, norm_mlp_pre, norm_mlp_post, w_up, w_down, w_ple_gate,
           w_ple_proj, norm_ple_post):
    b, s, _ = x.shape
    half = HEAD_DIM // 2
    invf = (ROPE_THETA ** (-jnp.arange(half, dtype=F32) * 2.0 / HEAD_DIM)).reshape(half, 1)
    pos3 = positions.reshape(b, 1, s)
    expand = (jnp.arange(LANES)[:, None] == jnp.arange(ATTN_WIDTH)[None, :] // HEAD_DIM).astype(BF16)
    row = lambda a: a.reshape(1, -1)

    h = x
    for i in range(w_in.shape[0]):
        wi = w_in[i].astype(BF16)
        c0 = 3 * ATTN_WIDTH
        c1 = c0 + SSD_WIDTH
        c2 = c1 + CONV_CH
        *qkv, z, xs, bm, cm, dt = _in_proj(
            h, pos3, invf, row(norm_mix_pre[i]),
            wi[:, 0:ATTN_WIDTH], wi[:, ATTN_WIDTH:2 * ATTN_WIDTH], wi[:, 2 * ATTN_WIDTH:c0],
            wi[:, c0:c1], wi[:, c1:c2], _pad_lanes(wi[:, c2:]),
            conv_w[i], row(conv_b[i]), _pad_lanes(row(dt_bias[i])))
        y = _ssd(xs, bm, cm, dt, z, _pad_lanes(row(a_log[i])),
                 row(jnp.repeat(d_skip[i], HEAD_DIM)), row(ssd_norm_g[i]))
        attn = _attention(qkv, expand)
        h = _tail(h.reshape(b * s, D_MODEL), attn.reshape(b * s, ATTN_WIDTH),
                  y.reshape(b * s, SSD_WIDTH), p[i].reshape(b * s, PLE_DIM),
                  row(norm_mix_post[i]), w_out[i].astype(BF16), row(norm_mlp_pre[i]),
                  w_up[i].astype(BF16), w_down[i].astype(BF16), row(norm_mlp_post[i]),
                  w_ple_gate[i].astype(BF16), w_ple_proj[i].astype(BF16),
                  row(norm_ple_post[i])).reshape(b, s, D_MODEL)
    return h
```

```python
import functools
import math

import jax
import jax.numpy as jnp
from jax import lax
from jax.experimental import pallas as pl
from jax.experimental.pallas import tpu as pltpu

D_MODEL = 1024
PLE_DIM = 256
N_HEADS = 8
HEAD_DIM = 64
ATTN_WIDTH = N_HEADS * HEAD_DIM
SSD_WIDTH = 512
SSD_STATE = 128
CONV_WIDTH = 4
CONV_CH = SSD_WIDTH + 2 * SSD_STATE
CHUNK = 128
D_FF = 4 * D_MODEL
ROPE_THETA = 10000.0
BRANCHES = ((128, 1), (512, 4), (2048, 16))
DILATIONS = tuple(d for _, d in BRANCHES)
ATTN_BLOCK = 128
NORM_EPS = 1e-6
Q_SCALE = HEAD_DIM ** -0.5 * math.log2(math.e)

LANES = 128
CONV_HALO = 8
PROJ_TILE = 512
TAIL_TILE = 512
FF_CHUNK = 1024
MERGE_ROWS = 256
ATTN_GROUP = ATTN_BLOCK * max(DILATIONS)
VMEM_LIMIT = 56 * 1024 * 1024

F32 = jnp.float32
BF16 = jnp.bfloat16


def _dot(a, b):
    return jnp.dot(a, b, preferred_element_type=F32)


def _dot_nt(a, b):
    return lax.dot_general(a, b, (((1,), (1,)), ((), ())), preferred_element_type=F32)


def _sigmoid(x):
    return 1.0 / (1.0 + jnp.exp(-x))


def _rms(x):
    return x * lax.rsqrt(jnp.mean(x * x, axis=-1, keepdims=True) + NORM_EPS)


def _in_proj_kernel(x_ref, pos_ref, invf_ref, g_ref, wq_ref, wk_ref, wv_ref, wz_ref,
                    wxbc_ref, wdt_ref, convw_ref, convb_ref, dtb_ref, *rest):
    n_qkv = 3 * len(DILATIONS)
    qkv_refs = rest[:n_qkv]
    z_ref, xs_ref, b_ref, c_ref, dt_ref, cbuf, qs, ks, vs = rest[n_qkv:n_qkv + 9]
    stages = rest[n_qkv + 9:]
    t = x_ref.shape[1]
    j = pl.program_id(1)
    u = (_rms(x_ref[0]) * g_ref[...]).astype(BF16)

    ang = invf_ref[...] * pos_ref[0].astype(F32)
    cos4 = jnp.concatenate([jnp.cos(ang)] * 4, axis=0).T
    sin4 = jnp.concatenate([jnp.sin(ang)] * 4, axis=0).T
    lane = lax.broadcasted_iota(jnp.int32, (t, LANES), 1)
    first_half = (lane % HEAD_DIM) < (HEAD_DIM // 2)
    sin_signed = jnp.where(first_half, -sin4, sin4)

    def rope(v):
        partner = jnp.where(first_half, pltpu.roll(v, LANES - HEAD_DIM // 2, 1),
                            pltpu.roll(v, HEAD_DIM // 2, 1))
        return v * cos4 + partner * sin_signed

    q = _dot(u, wq_ref[...])
    k = _dot(u, wk_ref[...])
    v = _dot(u, wv_ref[...])
    for g in range(ATTN_WIDTH // LANES):
        sl = slice(g * LANES, (g + 1) * LANES)
        qs[g] = rope(q[:, sl]) * Q_SCALE
        ks[g] = rope(k[:, sl])
        vs[g] = v[:, sl]

    planes, d_prev = (qs, ks, vs), 1
    for bi, d in enumerate(DILATIONS):
        f = d // d_prev
        staged = stages[3 * (bi - 1):3 * bi] if 0 < bi < len(DILATIONS) - 1 else (None,) * 3
        for src, stage, dst in zip(planes, staged, qkv_refs[3 * bi:3 * bi + 3]):
            for r in range(d):
                start = (r % d_prev) * (t // d_prev) + r // d_prev
                for g in range(ATTN_WIDTH // LANES):
                    val = src[g] if d == 1 else src[g, pl.ds(start, t // d, stride=f), :]
                    if stage is not None:
                        stage[g, r * (t // d):(r + 1) * (t // d), :] = val
                    c = r * ATTN_WIDTH + g * LANES
                    dst[0, :, c:c + LANES] = val.astype(BF16)
        if staged[0] is not None:
            planes, d_prev = staged, d

    z_ref[0] = _dot(u, wz_ref[...]).astype(BF16)

    xbc = _dot(u, wxbc_ref[...])

    @pl.when(j == 0)
    def _():
        cbuf[0:CONV_HALO, :] = jnp.zeros((CONV_HALO, CONV_CH), F32)

    cbuf[CONV_HALO:CONV_HALO + t, :] = xbc
    w = convw_ref[...]
    y = convb_ref[...] + w[CONV_WIDTH - 1:CONV_WIDTH] * xbc
    for s in range(1, CONV_WIDTH):
        y = y + w[CONV_WIDTH - 1 - s:CONV_WIDTH - s] * cbuf[CONV_HALO - s:CONV_HALO - s + t, :]
    cbuf[0:CONV_HALO, :] = cbuf[t:t + CONV_HALO, :]
    act = y * _sigmoid(y)
    xs_ref[0] = act[:, :SSD_WIDTH].astype(BF16)
    b_ref[0] = act[:, SSD_WIDTH:SSD_WIDTH + SSD_STATE].astype(BF16)
    c_ref[0] = act[:, SSD_WIDTH + SSD_STATE:].astype(BF16)

    dtr = _dot(u, wdt_ref[...]) + dtb_ref[...]
    dt_ref[0] = jnp.maximum(dtr, 0.0) + jnp.log1p(jnp.exp(-jnp.abs(dtr)))


def _in_proj(x, pos3, invf, g, wq, wk, wv, wz, wxbc, wdt, convw, convb, dtb):
    b, s, _ = x.shape
    t = PROJ_TILE
    tile = lambda w_: pl.BlockSpec((1, t, w_), lambda bi, j: (bi, j, 0))
    full = lambda a: pl.BlockSpec(a.shape, lambda bi, j: (0,) * a.ndim)
    wide = jax.ShapeDtypeStruct((b, s, ATTN_WIDTH), BF16)
    narrow = jax.ShapeDtypeStruct((b, s, SSD_STATE), BF16)
    qkv_specs, qkv_shapes = [], []
    for d in DILATIONS:
        qkv_specs += [pl.BlockSpec((1, t // d, d * ATTN_WIDTH), lambda bi, j: (bi, j, 0))] * 3
        qkv_shapes += [jax.ShapeDtypeStruct((b, s // d, d * ATTN_WIDTH), BF16)] * 3
    return pl.pallas_call(
        _in_proj_kernel,
        grid=(b, s // t),
        in_specs=[tile(D_MODEL), pl.BlockSpec((1, 1, t), lambda bi, j: (bi, 0, j)),
                  full(invf), full(g), full(wq), full(wk), full(wv), full(wz), full(wxbc),
                  full(wdt), full(convw), full(convb), full(dtb)],
        out_specs=qkv_specs + [tile(ATTN_WIDTH)] * 2 + [tile(SSD_STATE)] * 3,
        out_shape=qkv_shapes + [wide] * 2
                  + [narrow, narrow, jax.ShapeDtypeStruct((b, s, LANES), F32)],
        scratch_shapes=[pltpu.VMEM((t + CONV_HALO, CONV_CH), F32)]
                       + [pltpu.VMEM((ATTN_WIDTH // LANES, t, LANES), F32)] * (3 * (len(DILATIONS) - 1)),
        compiler_params=pltpu.CompilerParams(
            dimension_semantics=("arbitrary", "arbitrary"), vmem_limit_bytes=VMEM_LIMIT),
        name="in_proj",
    )(x, pos3, invf, g, wq, wk, wv, wz, wxbc, wdt, convw, convb, dtb)


def _expand_heads(m):
    rows = m.shape[0]
    lane = lax.broadcasted_iota(jnp.int32, (rows, LANES), 1)
    parts = []
    for g in range(SSD_WIDTH // LANES):
        lo = jnp.broadcast_to(m[:, 2 * g:2 * g + 1], (rows, LANES))
        hi = jnp.broadcast_to(m[:, 2 * g + 1:2 * g + 2], (rows, LANES))
        parts.append(jnp.where(lane < HEAD_DIM, lo, hi))
    return jnp.concatenate(parts, axis=1)


def _ssd_kernel(xs_ref, b_ref, c_ref, dt_ref, z_ref, alog_ref, dskip_ref, g_ref, y_ref, state):
    @pl.when(pl.program_id(1) == 0)
    def _():
        state[...] = jnp.zeros(state.shape, F32)

    row = lax.broadcasted_iota(jnp.int32, (CHUNK, CHUNK), 0)
    col = lax.broadcasted_iota(jnp.int32, (CHUNK, CHUNK), 1)
    causal = row >= col

    dt = dt_ref[0]
    adt = dt * (-jnp.exp(alog_ref[...]))
    cs = jnp.dot(causal.astype(F32), adt, preferred_element_type=F32,
                 precision=lax.Precision.HIGHEST)
    cs_t = cs.T
    xs = xs_ref[0].astype(F32)
    xdt = xs * _expand_heads(dt)
    xdt_b = xdt.astype(BF16)
    bm = b_ref[0]
    cm = c_ref[0]
    cb = _dot_nt(cm, bm)

    diag = []
    for g in range(SSD_WIDTH // LANES):
        xg = xdt_b[:, g * LANES:(g + 1) * LANES]
        halves = []
        for h in (2 * g, 2 * g + 1):
            seg = jnp.where(causal, cs[:, h:h + 1] - cs_t[h:h + 1, :], -jnp.inf)
            halves.append(_dot((cb * jnp.exp(seg)).astype(BF16), xg))
        diag.append(jnp.where(col < HEAD_DIM, halves[0], halves[1]))
    y = jnp.concatenate(diag, axis=1)

    prev = state[...]
    grow = _expand_heads(jnp.exp(cs))
    y = y + _dot(cm, prev.astype(BF16)) * grow
    to_end = _expand_heads(jnp.exp(cs[CHUNK - 1:CHUNK, :] - cs))
    inject = _dot(bm.astype(F32).T.astype(BF16), (xdt * to_end).astype(BF16))
    state[...] = prev * grow[CHUNK - 1:CHUNK, :] + inject

    y = y + dskip_ref[...] * xs
    zf = z_ref[0].astype(F32)
    y_ref[0] = (_rms(y * (zf * _sigmoid(zf))) * g_ref[...]).astype(BF16)


def _ssd(xs, bm, cm, dt, z, alog, dskip, g):
    b, s, _ = xs.shape
    blk = lambda w_: pl.BlockSpec((1, CHUNK, w_), lambda bi, c: (bi, c, 0))
    full = lambda a: pl.BlockSpec(a.shape, lambda bi, c: (0,) * a.ndim)
    return pl.pallas_call(
        _ssd_kernel,
        grid=(b, s // CHUNK),
        in_specs=[blk(SSD_WIDTH), blk(SSD_STATE), blk(SSD_STATE), blk(LANES), blk(SSD_WIDTH),
                  full(alog), full(dskip), full(g)],
        out_specs=blk(SSD_WIDTH),
        out_shape=jax.ShapeDtypeStruct((b, s, SSD_WIDTH), BF16),
        scratch_shapes=[pltpu.VMEM((SSD_STATE, SSD_WIDTH), F32)],
        compiler_params=pltpu.CompilerParams(
            dimension_semantics=("arbitrary", "arbitrary"), vmem_limit_bytes=VMEM_LIMIT),
        name="ssd",
    )(xs, bm, cm, dt, z, alog, dskip, g)


def _attn_block(has_prev, span, q_ref, kp_ref, kc_ref, vp_ref, vc_ref):
    blk = ATTN_BLOCK
    qi = lax.broadcasted_iota(jnp.int32, (blk, 2 * blk), 0)
    ki = lax.broadcasted_iota(jnp.int32, (blk, 2 * blk), 1)
    dist = qi + blk - ki
    valid = (dist >= 0) & (dist <= span) & ((ki >= blk) | has_prev)
    bias = jnp.where(valid, 0.0, -jnp.inf).astype(F32)
    bias = jnp.concatenate([bias, bias], axis=0)
    low = lax.broadcasted_iota(jnp.int32, (blk, LANES), 1) < HEAD_DIM

    q = q_ref[0]
    k = jnp.concatenate([kp_ref[0], kc_ref[0]], axis=0)
    v = jnp.concatenate([vp_ref[0], vc_ref[0]], axis=0)
    outs, maxes, sums = [], [], []
    for g in range(ATTN_WIDTH // LANES):
        sl = slice(g * LANES, (g + 1) * LANES)
        q2, k2, v2 = q[:, sl], k[:, sl], v[:, sl]
        zero = jnp.zeros_like(q2)
        qq = jnp.concatenate([jnp.where(low, q2, zero), jnp.where(low, zero, q2)], axis=0)
        s = _dot_nt(qq, k2) + bias
        m = jnp.max(s, axis=-1, keepdims=True)
        p = jnp.exp2(s - m)
        l = jnp.sum(p, axis=-1, keepdims=True)
        o = _dot(p.astype(BF16), v2)
        outs.append(jnp.where(low, o[:blk], o[blk:]))
        maxes.append(jnp.where(low, m[:blk], m[blk:]))
        sums.append(jnp.where(low, l[:blk], l[blk:]))
    return outs, maxes, sums


def _attn_kernel(steps, *refs):
    nbr = len(BRANCHES)
    in_refs = refs[:5 * nbr]
    out_ref = refs[5 * nbr]
    acc_o = refs[5 * nbr + 1:5 * nbr + 1 + nbr]
    acc_m = refs[5 * nbr + 1 + nbr:5 * nbr + 1 + 2 * nbr]
    acc_l = refs[5 * nbr + 1 + 2 * nbr:]
    i = pl.program_id(1)
    group_steps = ATTN_GROUP // ATTN_BLOCK
    local = i % group_steps

    for bi, (window, d) in enumerate(BRANCHES):
        r = i % d
        n = i // d
        o, m, l = _attn_block(n > 0, window // d, *in_refs[5 * bi:5 * bi + 5])
        start = (local // d) * (d * ATTN_BLOCK) + r
        rows = pl.ds(start, ATTN_BLOCK) if d == 1 else pl.ds(start, ATTN_BLOCK, stride=d)
        for g in range(ATTN_WIDTH // LANES):
            acc_o[bi][g, rows, :] = o[g]
            acc_m[bi][g, rows, :] = m[g]
            acc_l[bi][g, rows, :] = l[g]

    @pl.when(local == group_steps - 1)
    def _():
        total = lambda xs: functools.reduce(lambda a, b_: a + b_, xs)

        def merge(c, carry):
            rows = pl.ds(pl.multiple_of(c * MERGE_ROWS, MERGE_ROWS), MERGE_ROWS)
            for g in range(ATTN_WIDTH // LANES):
                ms = [a[g, rows, :] for a in acc_m]
                top = functools.reduce(jnp.maximum, ms)
                es = [jnp.exp2(m - top) for m in ms]
                den = total([e * a[g, rows, :] for e, a in zip(es, acc_l)])
                num = total([e * a[g, rows, :] for e, a in zip(es, acc_o)])
                out_ref[0, rows, g * LANES:(g + 1) * LANES] = (num / den).astype(BF16)
            return carry

        lax.fori_loop(0, ATTN_GROUP // MERGE_ROWS, merge, 0)


def _attention(qkv):
    b = qkv[0].shape[0]
    s = qkv[0].shape[1] * DILATIONS[0]
    steps = s // ATTN_BLOCK
    group_steps = ATTN_GROUP // ATTN_BLOCK
    in_specs = []
    for d in DILATIONS:
        cur = lambda bi, i, d=d: (bi, i // d, i % d)
        prev = lambda bi, i, d=d: (bi, jnp.maximum(i // d - 1, 0), i % d)
        wide = lambda m: pl.BlockSpec((1, ATTN_BLOCK, ATTN_WIDTH), m)
        in_specs += [wide(cur), wide(prev), wide(cur), wide(prev), wide(cur)]
    args = []
    for bi in range(len(DILATIONS)):
        q, k, v = qkv[3 * bi:3 * bi + 3]
        args += [q, k, k, v, v]
    nbr = len(BRANCHES)
    return pl.pallas_call(
        functools.partial(_attn_kernel, steps),
        grid=(b, steps),
        in_specs=in_specs,
        out_specs=pl.BlockSpec((1, ATTN_GROUP, ATTN_WIDTH), lambda bi, i: (bi, i // group_steps, 0)),
        out_shape=jax.ShapeDtypeStruct((b, s, ATTN_WIDTH), BF16),
        scratch_shapes=[pltpu.VMEM((ATTN_WIDTH // LANES, ATTN_GROUP, LANES), F32)] * (3 * nbr),
        compiler_params=pltpu.CompilerParams(
            dimension_semantics=("arbitrary", "arbitrary"), vmem_limit_bytes=VMEM_LIMIT),
        name="dilated_attention",
    )(*args)


def _tail_kernel(x_ref, attn_ref, y_ref, p_ref, gmix_ref, wout_ref, gpre_ref, wup_ref,
                 wdown_ref, gmlp_ref, wgate_ref, wproj_ref, gple_ref, out_ref):
    mix = (_dot(attn_ref[...], wout_ref[0:ATTN_WIDTH, :])
           + _dot(y_ref[...], wout_ref[ATTN_WIDTH:, :]))
    h = x_ref[...] + _rms(mix) * gmix_ref[...]

    u = (_rms(h) * gpre_ref[...]).astype(BF16)
    ff = jnp.zeros(h.shape, F32)
    for c in range(D_FF // FF_CHUNK):
        sl = slice(c * FF_CHUNK, (c + 1) * FF_CHUNK)
        hid = jnp.square(jnp.maximum(_dot(u, wup_ref[:, sl]), 0.0)).astype(BF16)
        ff = ff + _dot(hid, wdown_ref[sl, :])
    h = h + _rms(ff) * gmlp_ref[...]

    gate = _sigmoid(_dot(h.astype(BF16), wgate_ref[...]))
    ple = _dot(p_ref[...].astype(BF16), wproj_ref[...]) * gate
    out_ref[...] = h + _rms(ple) * gple_ref[...]


def _tail(x2, attn2, y2, p2, gmix, wout, gpre, wup, wdown, gmlp, wgate, wproj, gple):
    n = x2.shape[0]
    t = TAIL_TILE
    tile = lambda w_: pl.BlockSpec((t, w_), lambda i: (i, 0))
    full = lambda a: pl.BlockSpec(a.shape, lambda i: (0, 0), pipeline_mode=pl.Buffered(1))
    return pl.pallas_call(
        _tail_kernel,
        grid=(n // t,),
        in_specs=[tile(D_MODEL), tile(ATTN_WIDTH), tile(SSD_WIDTH), tile(PLE_DIM), full(gmix),
                  full(wout), full(gpre), full(wup), full(wdown), full(gmlp), full(wgate),
                  full(wproj), full(gple)],
        out_specs=tile(D_MODEL),
        out_shape=jax.ShapeDtypeStruct((n, D_MODEL), F32),
        compiler_params=pltpu.CompilerParams(
            dimension_semantics=("arbitrary",), vmem_limit_bytes=VMEM_LIMIT),
        name="tail",
    )(x2, attn2, y2, p2, gmix, wout, gpre, wup, wdown, gmlp, wgate, wproj, gple)


def _pad_lanes(a):
    return jnp.pad(a, ((0, 0), (0, LANES - a.shape[-1])))


def kernel(x, p, positions, norm_mix_pre, norm_mix_post, w_in, conv_w, conv_b, dt_bias, a_log,
           d_skip, ssd_norm_g, w_out, norm_mlp_pre, norm_mlp_post, w_up, w_down, w_ple_gate,
           w_ple_proj, norm_ple_post):
    b, s, _ = x.shape
    half = HEAD_DIM // 2
    invf = (ROPE_THETA ** (-jnp.arange(half, dtype=F32) * 2.0 / HEAD_DIM)).reshape(half, 1)
    pos3 = positions.reshape(b, 1, s)
    row = lambda a: a.reshape(1, -1)

    h = x
    for i in range(w_in.shape[0]):
        wi = w_in[i].astype(BF16)
        c0 = 3 * ATTN_WIDTH
        c1 = c0 + SSD_WIDTH
        c2 = c1 + CONV_CH
        *qkv, z, xs, bm, cm, dt = _in_proj(
            h, pos3, invf, row(norm_mix_pre[i]),
            wi[:, 0:ATTN_WIDTH], wi[:, ATTN_WIDTH:2 * ATTN_WIDTH], wi[:, 2 * ATTN_WIDTH:c0],
            wi[:, c0:c1], wi[:, c1:c2], _pad_lanes(wi[:, c2:]),
            conv_w[i], row(conv_b[i]), _pad_lanes(row(dt_bias[i])))
        y = _ssd(xs, bm, cm, dt, z, _pad_lanes(row(a_log[i])),
                 row(jnp.repeat(d_skip[i], HEAD_DIM)), row(ssd_norm_g[i]))
        attn = _attention(qkv)
        h = _tail(h.reshape(b * s, D_MODEL), attn.reshape(b * s, ATTN_WIDTH),
                  y.reshape(b * s, SSD_WIDTH), p[i].reshape(b * s, PLE_DIM),
                  row(norm_mix_post[i]), w_out[i].astype(BF16), row(norm_mlp_pre[i]),
                  w_up[i].astype(BF16), w_down[i].astype(BF16), row(norm_mlp_post[i]),
                  w_ple_gate[i].astype(BF16), w_ple_proj[i].astype(BF16),
                  row(norm_ple_post[i])).reshape(b, s, D_MODEL)
    return h
```

```python
import functools
import math

import jax
import jax.numpy as jnp
from jax import lax
from jax.experimental import pallas as pl
from jax.experimental.pallas import tpu as pltpu

D_MODEL = 1024
PLE_DIM = 256
N_HEADS = 8
HEAD_DIM = 64
ATTN_WIDTH = N_HEADS * HEAD_DIM
SSD_HEADS = 8
SSD_HEAD_DIM = 64
SSD_WIDTH = SSD_HEADS * SSD_HEAD_DIM
SSD_STATE = 128
CONV_WIDTH = 4
CONV_CH = SSD_WIDTH + 2 * SSD_STATE
CHUNK = 128
D_FF = 4 * D_MODEL
ROPE_THETA = 10000.0
BRANCHES = ((128, 1), (512, 4), (2048, 16))
DILATIONS = tuple(d for _, d in BRANCHES)
ATTN_BLOCK = 128
NORM_EPS = 1e-6
Q_SCALE = HEAD_DIM ** -0.5 * math.log2(math.e)

LANES = 128
CONV_HALO = 8
PROJ_TILE = 512
SSD_TILE = 512
TAIL_TILE = 512
FF_CHUNK = 1024
MERGE_ROWS = 256
ATTN_GROUP = ATTN_BLOCK * max(DILATIONS)
VMEM_LIMIT = 56 * 1024 * 1024

F32 = jnp.float32
BF16 = jnp.bfloat16


def _dot(a, b):
    return jnp.dot(a, b, preferred_element_type=F32)


def _dot_nt(a, b):
    return lax.dot_general(a, b, (((1,), (1,)), ((), ())), preferred_element_type=F32)


def _sigmoid(x):
    return 1.0 / (1.0 + jnp.exp(-x))


def _rms(x):
    return x * lax.rsqrt(jnp.mean(x * x, axis=-1, keepdims=True) + NORM_EPS)


def _in_proj_kernel(x_ref, pos_ref, invf_ref, g_ref, wq_ref, wk_ref, wv_ref, wz_ref,
                    wxbc_ref, wdt_ref, convw_ref, convb_ref, dtb_ref, *rest):
    n_qkv = 3 * len(DILATIONS)
    qkv_refs = rest[:n_qkv]
    z_ref, xs_ref, b_ref, c_ref, dt_ref, cbuf, qs, ks, vs = rest[n_qkv:n_qkv + 9]
    stages = rest[n_qkv + 9:]
    t = x_ref.shape[1]
    j = pl.program_id(1)
    u = (_rms(x_ref[0]) * g_ref[...]).astype(BF16)

    ang = invf_ref[...] * pos_ref[0].astype(F32)
    cos4 = jnp.concatenate([jnp.cos(ang)] * 4, axis=0).T
    sin4 = jnp.concatenate([jnp.sin(ang)] * 4, axis=0).T
    lane = lax.broadcasted_iota(jnp.int32, (t, LANES), 1)
    first_half = (lane % HEAD_DIM) < (HEAD_DIM // 2)
    sin_signed = jnp.where(first_half, -sin4, sin4)

    def rope(v):
        partner = jnp.where(first_half, pltpu.roll(v, LANES - HEAD_DIM // 2, 1),
                            pltpu.roll(v, HEAD_DIM // 2, 1))
        return v * cos4 + partner * sin_signed

    q = _dot(u, wq_ref[...])
    k = _dot(u, wk_ref[...])
    v = _dot(u, wv_ref[...])
    for g in range(ATTN_WIDTH // LANES):
        sl = slice(g * LANES, (g + 1) * LANES)
        qs[g] = rope(q[:, sl]) * Q_SCALE
        ks[g] = rope(k[:, sl])
        vs[g] = v[:, sl]

    planes, d_prev = (qs, ks, vs), 1
    for bi, d in enumerate(DILATIONS):
        f = d // d_prev
        staged = stages[3 * (bi - 1):3 * bi] if 0 < bi < len(DILATIONS) - 1 else (None,) * 3
        for src, stage, dst in zip(planes, staged, qkv_refs[3 * bi:3 * bi + 3]):
            for r in range(d):
                start = (r % d_prev) * (t // d_prev) + r // d_prev
                for g in range(ATTN_WIDTH // LANES):
                    val = src[g] if d == 1 else src[g, pl.ds(start, t // d, stride=f), :]
                    if stage is not None:
                        stage[g, r * (t // d):(r + 1) * (t // d), :] = val
                    c = r * ATTN_WIDTH + g * LANES
                    dst[0, :, c:c + LANES] = val.astype(BF16)
        if staged[0] is not None:
            planes, d_prev = staged, d

    z_ref[0] = _dot(u, wz_ref[...]).astype(BF16)

    xbc = _dot(u, wxbc_ref[...])

    @pl.when(j == 0)
    def _():
        cbuf[0:CONV_HALO, :] = jnp.zeros((CONV_HALO, CONV_CH), F32)

    cbuf[CONV_HALO:CONV_HALO + t, :] = xbc
    w = convw_ref[...]
    y = convb_ref[...] + w[CONV_WIDTH - 1:CONV_WIDTH] * xbc
    for s in range(1, CONV_WIDTH):
        y = y + w[CONV_WIDTH - 1 - s:CONV_WIDTH - s] * cbuf[CONV_HALO - s:CONV_HALO - s + t, :]
    cbuf[0:CONV_HALO, :] = cbuf[t:t + CONV_HALO, :]
    act = y * _sigmoid(y)
    xs_ref[0] = act[:, :SSD_WIDTH].astype(BF16)
    b_ref[0] = act[:, SSD_WIDTH:SSD_WIDTH + SSD_STATE].astype(BF16)
    c_ref[0] = act[:, SSD_WIDTH + SSD_STATE:].astype(BF16)

    dtr = _dot(u, wdt_ref[...]) + dtb_ref[...]
    dt_ref[0] = jnp.maximum(dtr, 0.0) + jnp.log1p(jnp.exp(-jnp.abs(dtr)))


def _in_proj(x, pos3, invf, g, wq, wk, wv, wz, wxbc, wdt, convw, convb, dtb):
    b, s, _ = x.shape
    t = PROJ_TILE
    tile = lambda w_: pl.BlockSpec((1, t, w_), lambda bi, j: (bi, j, 0))
    full = lambda a: pl.BlockSpec(a.shape, lambda bi, j: (0,) * a.ndim)
    wide = jax.ShapeDtypeStruct((b, s, ATTN_WIDTH), BF16)
    narrow = jax.ShapeDtypeStruct((b, s, SSD_STATE), BF16)
    qkv_specs, qkv_shapes = [], []
    for d in DILATIONS:
        qkv_specs += [pl.BlockSpec((1, t // d, d * ATTN_WIDTH), lambda bi, j: (bi, j, 0))] * 3
        qkv_shapes += [jax.ShapeDtypeStruct((b, s // d, d * ATTN_WIDTH), BF16)] * 3
    return pl.pallas_call(
        _in_proj_kernel,
        grid=(b, s // t),
        in_specs=[tile(D_MODEL), pl.BlockSpec((1, 1, t), lambda bi, j: (bi, 0, j)),
                  full(invf), full(g), full(wq), full(wk), full(wv), full(wz), full(wxbc),
                  full(wdt), full(convw), full(convb), full(dtb)],
        out_specs=qkv_specs + [tile(ATTN_WIDTH)] * 2 + [tile(SSD_STATE)] * 3,
        out_shape=qkv_shapes + [wide] * 2
                  + [narrow, narrow, jax.ShapeDtypeStruct((b, s, LANES), F32)],
        scratch_shapes=[pltpu.VMEM((t + CONV_HALO, CONV_CH), F32)]
                       + [pltpu.VMEM((ATTN_WIDTH // LANES, t, LANES), F32)] * (3 * (len(DILATIONS) - 1)),
        compiler_params=pltpu.CompilerParams(
            dimension_semantics=("arbitrary", "arbitrary"), vmem_limit_bytes=VMEM_LIMIT),
        name="in_proj",
    )(x, pos3, invf, g, wq, wk, wv, wz, wxbc, wdt, convw, convb, dtb)


def _split3(a):
    hi = a.astype(BF16)
    rest = a - hi.astype(F32)
    mid = rest.astype(BF16)
    return hi, mid, (rest - mid.astype(F32)).astype(BF16)


def _ssd_prepare(bm, cm, dt, a_row, a_col, tril):
    cs = sum(_dot(tril, piece) for piece in _split3(dt * a_row))
    dt_t = dt.T[0:2 * SSD_HEADS]
    cs_t = sum(_dot_nt(piece, tril) for piece in _split3(dt_t * a_col))
    w_t = dt_t * jnp.exp2(cs_t[:, CHUNK - 1:CHUNK] - cs_t)
    return dict(cs=cs, cs_dt_t=cs_t - jnp.log2(dt_t), w_t=w_t, cb=_dot_nt(cm, bm),
                cm_f=cm.astype(F32), bm_t=bm.astype(F32).T)


def _ssd_chunk(xs_b, pre, dskip, prev, causal, low):
    zero = jnp.zeros((CHUNK, LANES), BF16)
    ys, nxt = [], []
    for g in range(SSD_WIDTH // LANES):
        x2 = xs_b[:, g * LANES:(g + 1) * LANES]
        rhs = jnp.concatenate([x2, prev[g].astype(BF16)], axis=0)
        y_lhs, inj_lhs, dec2 = [], [], []
        for h in (2 * g, 2 * g + 1):
            cs_h = jnp.broadcast_to(pre["cs"][:, h:h + 1], (CHUNK, CHUNK))
            grow = jnp.exp2(cs_h)
            seg = jnp.exp2(jnp.where(causal, cs_h - pre["cs_dt_t"][h:h + 1, :], -jnp.inf))
            y_lhs.append(jnp.concatenate([pre["cb"] * seg, pre["cm_f"] * grow],
                                         axis=1).astype(BF16))
            inj_lhs.append((pre["bm_t"] * pre["w_t"][h:h + 1, :]).astype(BF16))
            dec2.append(grow[CHUNK - 1:CHUNK, :])
        y_pair = _dot(jnp.concatenate(y_lhs, axis=0), rhs)
        ys.append(jnp.where(low, y_pair[:CHUNK], y_pair[CHUNK:]))
        x_split = jnp.concatenate([jnp.where(low, x2, zero), jnp.where(low, zero, x2)], axis=0)
        inject = _dot(jnp.concatenate(inj_lhs, axis=1), x_split)
        nxt.append(prev[g] * jnp.where(low[0:1], dec2[0], dec2[1]) + inject)

    return (jnp.concatenate(ys, axis=1) + dskip * xs_b.astype(F32)).astype(BF16), nxt


def _ssd_kernel(xs_ref, b_ref, c_ref, dt_ref, alog_ref, alogc_ref, dskip_ref, y_ref, state):
    @pl.when(pl.program_id(1) == 0)
    def _():
        state[...] = jnp.zeros(state.shape, F32)

    row = lax.broadcasted_iota(jnp.int32, (CHUNK, CHUNK), 0)
    col = lax.broadcasted_iota(jnp.int32, (CHUNK, CHUNK), 1)
    causal = row >= col
    low = col < SSD_HEAD_DIM
    tril = causal.astype(BF16)
    a_row = -jnp.exp(alog_ref[...]) * math.log2(math.e)
    a_col = -jnp.exp(alogc_ref[...]) * math.log2(math.e)
    chunks = [slice(c * CHUNK, (c + 1) * CHUNK) for c in range(xs_ref.shape[1] // CHUNK)]
    groups = SSD_WIDTH // LANES
    st = [state[:, g * LANES:(g + 1) * LANES] for g in range(groups)]
    for rows in chunks:
        pre_c = _ssd_prepare(b_ref[0, rows, :], c_ref[0, rows, :], dt_ref[0, rows, :], a_row,
                             a_col, tril)
        y, st = _ssd_chunk(xs_ref[0, rows, :], pre_c, dskip_ref[...], st, causal, low)
        y_ref[0, rows, :] = y
    for g in range(groups):
        state[:, g * LANES:(g + 1) * LANES] = st[g]


def _ssd(xs, bm, cm, dt, alog, alogc, dskip):
    b, s, _ = xs.shape
    blk = lambda w_: pl.BlockSpec((1, SSD_TILE, w_), lambda bi, c: (bi, c, 0))
    full = lambda a: pl.BlockSpec(a.shape, lambda bi, c: (0,) * a.ndim)
    return pl.pallas_call(
        _ssd_kernel,
        grid=(b, s // SSD_TILE),
        in_specs=[blk(SSD_WIDTH), blk(SSD_STATE), blk(SSD_STATE), blk(LANES),
                  full(alog), full(alogc), full(dskip)],
        out_specs=blk(SSD_WIDTH),
        out_shape=jax.ShapeDtypeStruct((b, s, SSD_WIDTH), BF16),
        scratch_shapes=[pltpu.VMEM((SSD_STATE, SSD_WIDTH), F32)],
        compiler_params=pltpu.CompilerParams(
            dimension_semantics=("arbitrary", "arbitrary"), vmem_limit_bytes=VMEM_LIMIT),
        name="ssd",
    )(xs, bm, cm, dt, alog, alogc, dskip)


def _attn_block(has_prev, span, q_ref, kp_ref, kc_ref, vp_ref, vc_ref):
    blk = ATTN_BLOCK
    qi = lax.broadcasted_iota(jnp.int32, (blk, 2 * blk), 0)
    ki = lax.broadcasted_iota(jnp.int32, (blk, 2 * blk), 1)
    dist = qi + blk - ki
    valid = (dist >= 0) & (dist <= span) & ((ki >= blk) | has_prev)
    bias = jnp.where(valid, 0.0, -jnp.inf).astype(F32)
    bias = jnp.concatenate([bias, bias], axis=0)
    low = lax.broadcasted_iota(jnp.int32, (blk, LANES), 1) < HEAD_DIM

    q = q_ref[0]
    k = jnp.concatenate([kp_ref[0], kc_ref[0]], axis=0)
    v = jnp.concatenate([vp_ref[0], vc_ref[0]], axis=0)
    outs, maxes, sums = [], [], []
    for g in range(ATTN_WIDTH // LANES):
        sl = slice(g * LANES, (g + 1) * LANES)
        q2, k2, v2 = q[:, sl], k[:, sl], v[:, sl]
        zero = jnp.zeros_like(q2)
        qq = jnp.concatenate([jnp.where(low, q2, zero), jnp.where(low, zero, q2)], axis=0)
        s = _dot_nt(qq, k2) + bias
        m = jnp.max(s, axis=-1, keepdims=True)
        p = jnp.exp2(s - m)
        l = jnp.sum(p, axis=-1, keepdims=True)
        o = _dot(p.astype(BF16), v2)
        outs.append(jnp.where(low, o[:blk], o[blk:]))
        maxes.append(jnp.where(low, m[:blk], m[blk:]))
        sums.append(jnp.where(low, l[:blk], l[blk:]))
    return outs, maxes, sums


def _attn_kernel(steps, *refs):
    nbr = len(BRANCHES)
    in_refs = refs[:5 * nbr]
    out_ref = refs[5 * nbr]
    acc_o = refs[5 * nbr + 1:5 * nbr + 1 + nbr]
    acc_m = refs[5 * nbr + 1 + nbr:5 * nbr + 1 + 2 * nbr]
    acc_l = refs[5 * nbr + 1 + 2 * nbr:]
    i = pl.program_id(1)
    group_steps = ATTN_GROUP // ATTN_BLOCK
    local = i % group_steps

    for bi, (window, d) in enumerate(BRANCHES):
        r = i % d
        n = i // d
        o, m, l = _attn_block(n > 0, window // d, *in_refs[5 * bi:5 * bi + 5])
        start = (local // d) * (d * ATTN_BLOCK) + r
        rows = pl.ds(start, ATTN_BLOCK) if d == 1 else pl.ds(start, ATTN_BLOCK, stride=d)
        for g in range(ATTN_WIDTH // LANES):
            acc_o[bi][g, rows, :] = o[g]
            acc_m[bi][g, rows, :] = m[g]
            acc_l[bi][g, rows, :] = l[g]

    @pl.when(local == group_steps - 1)
    def _():
        total = lambda xs: functools.reduce(lambda a, b_: a + b_, xs)

        def merge(c, carry):
            rows = pl.ds(pl.multiple_of(c * MERGE_ROWS, MERGE_ROWS), MERGE_ROWS)
            for g in range(ATTN_WIDTH // LANES):
                ms = [a[g, rows, :] for a in acc_m]
                top = functools.reduce(jnp.maximum, ms)
                es = [jnp.exp2(m - top) for m in ms]
                den = total([e * a[g, rows, :] for e, a in zip(es, acc_l)])
                num = total([e * a[g, rows, :] for e, a in zip(es, acc_o)])
                out_ref[0, rows, g * LANES:(g + 1) * LANES] = (num / den).astype(BF16)
            return carry

        lax.fori_loop(0, ATTN_GROUP // MERGE_ROWS, merge, 0)


def _attention(qkv):
    b = qkv[0].shape[0]
    s = qkv[0].shape[1] * DILATIONS[0]
    steps = s // ATTN_BLOCK
    group_steps = ATTN_GROUP // ATTN_BLOCK
    in_specs = []
    for d in DILATIONS:
        cur = lambda bi, i, d=d: (bi, i // d, i % d)
        prev = lambda bi, i, d=d: (bi, jnp.maximum(i // d - 1, 0), i % d)
        wide = lambda m: pl.BlockSpec((1, ATTN_BLOCK, ATTN_WIDTH), m)
        in_specs += [wide(cur), wide(prev), wide(cur), wide(prev), wide(cur)]
    args = []
    for bi in range(len(DILATIONS)):
        q, k, v = qkv[3 * bi:3 * bi + 3]
        args += [q, k, k, v, v]
    nbr = len(BRANCHES)
    return pl.pallas_call(
        functools.partial(_attn_kernel, steps),
        grid=(b, steps),
        in_specs=in_specs,
        out_specs=pl.BlockSpec((1, ATTN_GROUP, ATTN_WIDTH), lambda bi, i: (bi, i // group_steps, 0)),
        out_shape=jax.ShapeDtypeStruct((b, s, ATTN_WIDTH), BF16),
        scratch_shapes=[pltpu.VMEM((ATTN_WIDTH // LANES, ATTN_GROUP, LANES), F32)] * (3 * nbr),
        compiler_params=pltpu.CompilerParams(
            dimension_semantics=("arbitrary", "arbitrary"), vmem_limit_bytes=VMEM_LIMIT),
        name="dilated_attention",
    )(*args)


def _tail_kernel(x_ref, attn_ref, y_ref, z_ref, p_ref, gssd_ref, gmix_ref, wout_ref, gpre_ref,
                 wup_ref, wdown_ref, gmlp_ref, wgate_ref, wproj_ref, gple_ref, out_ref):
    zf = z_ref[...].astype(F32)
    y = (_rms(y_ref[...].astype(F32) * (zf * _sigmoid(zf))) * gssd_ref[...]).astype(BF16)
    mix = _dot(attn_ref[...], wout_ref[0:ATTN_WIDTH, :]) + _dot(y, wout_ref[ATTN_WIDTH:, :])
    h = x_ref[...] + _rms(mix) * gmix_ref[...]

    u = (_rms(h) * gpre_ref[...]).astype(BF16)
    ff = jnp.zeros(h.shape, F32)
    for c in range(D_FF // FF_CHUNK):
        sl = slice(c * FF_CHUNK, (c + 1) * FF_CHUNK)
        hid = jnp.square(jnp.maximum(_dot(u, wup_ref[:, sl]), 0.0)).astype(BF16)
        ff = ff + _dot(hid, wdown_ref[sl, :])
    h = h + _rms(ff) * gmlp_ref[...]

    gate = _sigmoid(_dot(h.astype(BF16), wgate_ref[...]))
    ple = _dot(p_ref[...].astype(BF16), wproj_ref[...]) * gate
    out_ref[...] = h + _rms(ple) * gple_ref[...]


def _tail(x2, attn2, y2, z2, p2, gssd, gmix, wout, gpre, wup, wdown, gmlp, wgate, wproj, gple):
    n = x2.shape[0]
    t = TAIL_TILE
    tile = lambda w_: pl.BlockSpec((t, w_), lambda i: (i, 0))
    full = lambda a: pl.BlockSpec(a.shape, lambda i: (0, 0), pipeline_mode=pl.Buffered(1))
    return pl.pallas_call(
        _tail_kernel,
        grid=(n // t,),
        in_specs=[tile(D_MODEL), tile(ATTN_WIDTH), tile(SSD_WIDTH), tile(SSD_WIDTH),
                  tile(PLE_DIM), full(gssd), full(gmix),
                  full(wout), full(gpre), full(wup), full(wdown), full(gmlp), full(wgate),
                  full(wproj), full(gple)],
        out_specs=tile(D_MODEL),
        out_shape=jax.ShapeDtypeStruct((n, D_MODEL), F32),
        compiler_params=pltpu.CompilerParams(
            dimension_semantics=("arbitrary",), vmem_limit_bytes=VMEM_LIMIT),
        name="tail",
    )(x2, attn2, y2, z2, p2, gssd, gmix, wout, gpre, wup, wdown, gmlp, wgate, wproj, gple)


def _pad_lanes(a):
    return jnp.pad(a, ((0, 0), (0, LANES - a.shape[-1])))


def kernel(x, p, positions, norm_mix_pre, norm_mix_post, w_in, conv_w, conv_b, dt_bias, a_log,
           d_skip, ssd_norm_g, w_out, norm_mlp_pre, norm_mlp_post, w_up, w_down, w_ple_gate,
           w_ple_proj, norm_ple_post):
    b, s, _ = x.shape
    half = HEAD_DIM // 2
    invf = (ROPE_THETA ** (-jnp.arange(half, dtype=F32) * 2.0 / HEAD_DIM)).reshape(half, 1)
    pos3 = positions.reshape(b, 1, s)
    row = lambda a: a.reshape(1, -1)

    h = x
    for i in range(w_in.shape[0]):
        wi = w_in[i].astype(BF16)
        c0 = 3 * ATTN_WIDTH
        c1 = c0 + SSD_WIDTH
        c2 = c1 + CONV_CH
        *qkv, z, xs, bm, cm, dt = _in_proj(
            h, pos3, invf, row(norm_mix_pre[i]),
            wi[:, 0:ATTN_WIDTH], wi[:, ATTN_WIDTH:2 * ATTN_WIDTH], wi[:, 2 * ATTN_WIDTH:c0],
            wi[:, c0:c1], wi[:, c1:c2], _pad_lanes(wi[:, c2:]),
            conv_w[i], row(conv_b[i]), _pad_lanes(row(dt_bias[i])))
        y = _ssd(xs, bm, cm, dt, _pad_lanes(row(a_log[i])),
                 jnp.pad(a_log[i], (0, SSD_HEADS)).reshape(2 * SSD_HEADS, 1),
                 row(jnp.repeat(d_skip[i], SSD_HEAD_DIM)))
        attn = _attention(qkv)
        h = _tail(h.reshape(b * s, D_MODEL), attn.reshape(b * s, ATTN_WIDTH),
                  y.reshape(b * s, SSD_WIDTH), z.reshape(b * s, SSD_WIDTH),
                  p[i].reshape(b * s, PLE_DIM), row(ssd_norm_g[i]),
                  row(norm_mix_post[i]), w_out[i].astype(BF16), row(norm_mlp_pre[i]),
                  w_up[i].astype(BF16), w_down[i].astype(BF16), row(norm_mlp_post[i]),
                  w_ple_gate[i].astype(BF16), w_ple_proj[i].astype(BF16),
                  row(norm_ple_post[i])).reshape(b, s, D_MODEL)
    return h
```

```python
import functools
import math

import jax
import jax.numpy as jnp
from jax import lax
from jax.experimental import pallas as pl
from jax.experimental.pallas import tpu as pltpu

D_MODEL = 1024
PLE_DIM = 256
N_HEADS = 8
HEAD_DIM = 64
ATTN_WIDTH = N_HEADS * HEAD_DIM
SSD_HEADS = 8
SSD_HEAD_DIM = 64
SSD_WIDTH = SSD_HEADS * SSD_HEAD_DIM
SSD_STATE = 128
CONV_WIDTH = 4
CONV_CH = SSD_WIDTH + 2 * SSD_STATE
CHUNK = 128
D_FF = 4 * D_MODEL
ROPE_THETA = 10000.0
BRANCHES = ((128, 1), (512, 4), (2048, 16))
DILATIONS = tuple(d for _, d in BRANCHES)
ATTN_BLOCK = 128
NORM_EPS = 1e-6
Q_SCALE = HEAD_DIM ** -0.5 * math.log2(math.e)

LANES = 128
CONV_HALO = 8
PROJ_TILE = 512
TAIL_TILE = 512
FF_CHUNK = 1024
MERGE_ROWS = 256
ATTN_GROUP = ATTN_BLOCK * max(DILATIONS)
VMEM_LIMIT = 56 * 1024 * 1024

F32 = jnp.float32
BF16 = jnp.bfloat16


def _dot(a, b):
    return jnp.dot(a, b, preferred_element_type=F32)


def _dot_nt(a, b):
    return lax.dot_general(a, b, (((1,), (1,)), ((), ())), preferred_element_type=F32)


def _sigmoid(x):
    return 1.0 / (1.0 + jnp.exp(-x))


def _rms(x):
    return x * lax.rsqrt(jnp.mean(x * x, axis=-1, keepdims=True) + NORM_EPS)


def _split3(a):
    hi = a.astype(BF16)
    rest = a - hi.astype(F32)
    mid = rest.astype(BF16)
    return hi, mid, (rest - mid.astype(F32)).astype(BF16)


def _ssd_prepare(bm, cm, dt, a_row, a_col, tril):
    cs = sum(_dot(tril, piece) for piece in _split3(dt * a_row))
    dt_t = dt.T[0:2 * SSD_HEADS]
    cs_t = sum(_dot_nt(piece, tril) for piece in _split3(dt_t * a_col))
    w_t = dt_t * jnp.exp2(cs_t[:, CHUNK - 1:CHUNK] - cs_t)
    return dict(cs=cs, cs_dt_t=cs_t - jnp.log2(dt_t), w_t=w_t, cb=_dot_nt(cm, bm),
                cm_f=cm.astype(F32), bm_t=bm.astype(F32).T)


def _ssd_chunk(xs_b, pre, dskip, prev, causal, low):
    zero = jnp.zeros((CHUNK, LANES), BF16)
    ys, nxt = [], []
    for g in range(SSD_WIDTH // LANES):
        x2 = xs_b[:, g * LANES:(g + 1) * LANES]
        rhs = jnp.concatenate([x2, prev[g].astype(BF16)], axis=0)
        y_lhs, inj_lhs, dec2 = [], [], []
        for h in (2 * g, 2 * g + 1):
            cs_h = jnp.broadcast_to(pre["cs"][:, h:h + 1], (CHUNK, CHUNK))
            grow = jnp.exp2(cs_h)
            seg = jnp.exp2(jnp.where(causal, cs_h - pre["cs_dt_t"][h:h + 1, :], -jnp.inf))
            y_lhs.append(jnp.concatenate([pre["cb"] * seg, pre["cm_f"] * grow],
                                         axis=1).astype(BF16))
            inj_lhs.append((pre["bm_t"] * pre["w_t"][h:h + 1, :]).astype(BF16))
            dec2.append(grow[CHUNK - 1:CHUNK, :])
        y_pair = _dot(jnp.concatenate(y_lhs, axis=0), rhs)
        ys.append(jnp.where(low, y_pair[:CHUNK], y_pair[CHUNK:]))
        x_split = jnp.concatenate([jnp.where(low, x2, zero), jnp.where(low, zero, x2)], axis=0)
        inject = _dot(jnp.concatenate(inj_lhs, axis=1), x_split)
        nxt.append(prev[g] * jnp.where(low[0:1], dec2[0], dec2[1]) + inject)

    return (jnp.concatenate(ys, axis=1) + dskip * xs_b.astype(F32)).astype(BF16), nxt


def _in_proj_kernel(x_ref, pos_ref, invf_ref, g_ref, wq_ref, wk_ref, wv_ref, wz_ref,
                    wxbc_ref, wdt_ref, convw_ref, convb_ref, dtb_ref, alog_ref, alogc_ref,
                    dskip_ref, *rest):
    n_qkv = 3 * len(DILATIONS)
    qkv_refs = rest[:n_qkv]
    z_ref, y_ref, cbuf, qs, ks, vs = rest[n_qkv:n_qkv + 6]
    stages, state = rest[n_qkv + 6:-1], rest[-1]
    t = x_ref.shape[1]
    j = pl.program_id(1)
    u = (_rms(x_ref[0]) * g_ref[...]).astype(BF16)

    ang = invf_ref[...] * pos_ref[0].astype(F32)
    cos4 = jnp.concatenate([jnp.cos(ang)] * 4, axis=0).T
    sin4 = jnp.concatenate([jnp.sin(ang)] * 4, axis=0).T
    lane = lax.broadcasted_iota(jnp.int32, (t, LANES), 1)
    first_half = (lane % HEAD_DIM) < (HEAD_DIM // 2)
    sin_signed = jnp.where(first_half, -sin4, sin4)

    def rope(v):
        partner = jnp.where(first_half, pltpu.roll(v, LANES - HEAD_DIM // 2, 1),
                            pltpu.roll(v, HEAD_DIM // 2, 1))
        return v * cos4 + partner * sin_signed

    q = _dot(u, wq_ref[...])
    k = _dot(u, wk_ref[...])
    v = _dot(u, wv_ref[...])
    for g in range(ATTN_WIDTH // LANES):
        sl = slice(g * LANES, (g + 1) * LANES)
        qs[g] = rope(q[:, sl]) * Q_SCALE
        ks[g] = rope(k[:, sl])
        vs[g] = v[:, sl]

    planes, d_prev = (qs, ks, vs), 1
    for bi, d in enumerate(DILATIONS):
        f = d // d_prev
        staged = stages[3 * (bi - 1):3 * bi] if 0 < bi < len(DILATIONS) - 1 else (None,) * 3
        for src, stage, dst in zip(planes, staged, qkv_refs[3 * bi:3 * bi + 3]):
            for r in range(d):
                start = (r % d_prev) * (t // d_prev) + r // d_prev
                for g in range(ATTN_WIDTH // LANES):
                    val = src[g] if d == 1 else src[g, pl.ds(start, t // d, stride=f), :]
                    if stage is not None:
                        stage[g, r * (t // d):(r + 1) * (t // d), :] = val
                    c = r * ATTN_WIDTH + g * LANES
                    dst[0, :, c:c + LANES] = val.astype(BF16)
        if staged[0] is not None:
            planes, d_prev = staged, d

    z_ref[0] = _dot(u, wz_ref[...]).astype(BF16)

    xbc = _dot(u, wxbc_ref[...])

    @pl.when(j == 0)
    def _():
        cbuf[0:CONV_HALO, :] = jnp.zeros((CONV_HALO, CONV_CH), F32)
        state[...] = jnp.zeros(state.shape, F32)

    cbuf[CONV_HALO:CONV_HALO + t, :] = xbc
    w = convw_ref[...]
    y = convb_ref[...] + w[CONV_WIDTH - 1:CONV_WIDTH] * xbc
    for s in range(1, CONV_WIDTH):
        y = y + w[CONV_WIDTH - 1 - s:CONV_WIDTH - s] * cbuf[CONV_HALO - s:CONV_HALO - s + t, :]
    cbuf[0:CONV_HALO, :] = cbuf[t:t + CONV_HALO, :]
    act = y * _sigmoid(y)
    xs_b = act[:, :SSD_WIDTH].astype(BF16)
    bm = act[:, SSD_WIDTH:SSD_WIDTH + SSD_STATE].astype(BF16)
    cm = act[:, SSD_WIDTH + SSD_STATE:].astype(BF16)

    dtr = _dot(u, wdt_ref[...]) + dtb_ref[...]
    dt = jnp.maximum(dtr, 0.0) + jnp.log1p(jnp.exp(-jnp.abs(dtr)))

    row = lax.broadcasted_iota(jnp.int32, (CHUNK, CHUNK), 0)
    col = lax.broadcasted_iota(jnp.int32, (CHUNK, CHUNK), 1)
    causal = row >= col
    low = col < SSD_HEAD_DIM
    tril = causal.astype(BF16)
    a_row = -jnp.exp(alog_ref[...]) * math.log2(math.e)
    a_col = -jnp.exp(alogc_ref[...]) * math.log2(math.e)
    groups = SSD_WIDTH // LANES
    st = [state[:, g * LANES:(g + 1) * LANES] for g in range(groups)]
    for c in range(t // CHUNK):
        rows = slice(c * CHUNK, (c + 1) * CHUNK)
        pre_c = _ssd_prepare(bm[rows], cm[rows], dt[rows], a_row, a_col, tril)
        yc, st = _ssd_chunk(xs_b[rows], pre_c, dskip_ref[...], st, causal, low)
        y_ref[0, rows, :] = yc
    for g in range(groups):
        state[:, g * LANES:(g + 1) * LANES] = st[g]


def _in_proj(x, pos3, invf, g, wq, wk, wv, wz, wxbc, wdt, convw, convb, dtb, alog, alogc, dskip):
    b, s, _ = x.shape
    t = PROJ_TILE
    tile = lambda w_: pl.BlockSpec((1, t, w_), lambda bi, j: (bi, j, 0))
    full = lambda a: pl.BlockSpec(a.shape, lambda bi, j: (0,) * a.ndim)
    wide = jax.ShapeDtypeStruct((b, s, ATTN_WIDTH), BF16)
    qkv_specs, qkv_shapes = [], []
    for d in DILATIONS:
        qkv_specs += [pl.BlockSpec((1, t // d, d * ATTN_WIDTH), lambda bi, j: (bi, j, 0))] * 3
        qkv_shapes += [jax.ShapeDtypeStruct((b, s // d, d * ATTN_WIDTH), BF16)] * 3
    plane = pltpu.VMEM((ATTN_WIDTH // LANES, t, LANES), F32)
    return pl.pallas_call(
        _in_proj_kernel,
        grid=(b, s // t),
        in_specs=[tile(D_MODEL), pl.BlockSpec((1, 1, t), lambda bi, j: (bi, 0, j)),
                  full(invf), full(g), full(wq), full(wk), full(wv), full(wz), full(wxbc),
                  full(wdt), full(convw), full(convb), full(dtb), full(alog), full(alogc),
                  full(dskip)],
        out_specs=qkv_specs + [tile(ATTN_WIDTH), tile(SSD_WIDTH)],
        out_shape=qkv_shapes + [wide, wide],
        scratch_shapes=[pltpu.VMEM((t + CONV_HALO, CONV_CH), F32)]
                       + [plane] * (3 * (len(DILATIONS) - 1))
                       + [pltpu.VMEM((SSD_STATE, SSD_WIDTH), F32)],
        compiler_params=pltpu.CompilerParams(
            dimension_semantics=("arbitrary", "arbitrary"), vmem_limit_bytes=VMEM_LIMIT),
        name="in_proj",
    )(x, pos3, invf, g, wq, wk, wv, wz, wxbc, wdt, convw, convb, dtb, alog, alogc, dskip)


def _attn_block(has_prev, span, q_ref, kp_ref, kc_ref, vp_ref, vc_ref):
    blk = ATTN_BLOCK
    qi = lax.broadcasted_iota(jnp.int32, (blk, 2 * blk), 0)
    ki = lax.broadcasted_iota(jnp.int32, (blk, 2 * blk), 1)
    dist = qi + blk - ki
    valid = (dist >= 0) & (dist <= span) & ((ki >= blk) | has_prev)
    bias = jnp.where(valid, 0.0, -jnp.inf).astype(F32)
    bias = jnp.concatenate([bias, bias], axis=0)
    low = lax.broadcasted_iota(jnp.int32, (blk, LANES), 1) < HEAD_DIM

    q = q_ref[0]
    k = jnp.concatenate([kp_ref[0], kc_ref[0]], axis=0)
    v = jnp.concatenate([vp_ref[0], vc_ref[0]], axis=0)
    outs, maxes, sums = [], [], []
    for g in range(ATTN_WIDTH // LANES):
        sl = slice(g * LANES, (g + 1) * LANES)
        q2, k2, v2 = q[:, sl], k[:, sl], v[:, sl]
        zero = jnp.zeros_like(q2)
        qq = jnp.concatenate([jnp.where(low, q2, zero), jnp.where(low, zero, q2)], axis=0)
        s = _dot_nt(qq, k2) + bias
        m = jnp.max(s, axis=-1, keepdims=True)
        p = jnp.exp2(s - m)
        l = jnp.sum(p, axis=-1, keepdims=True)
        o = _dot(p.astype(BF16), v2)
        outs.append(jnp.where(low, o[:blk], o[blk:]))
        maxes.append(jnp.where(low, m[:blk], m[blk:]))
        sums.append(jnp.where(low, l[:blk], l[blk:]))
    return outs, maxes, sums


def _attn_kernel(steps, *refs):
    nbr = len(BRANCHES)
    in_refs = refs[:5 * nbr]
    out_ref = refs[5 * nbr]
    acc_o = refs[5 * nbr + 1:5 * nbr + 1 + nbr]
    acc_m = refs[5 * nbr + 1 + nbr:5 * nbr + 1 + 2 * nbr]
    acc_l = refs[5 * nbr + 1 + 2 * nbr:]
    i = pl.program_id(1)
    group_steps = ATTN_GROUP // ATTN_BLOCK
    local = i % group_steps

    for bi, (window, d) in enumerate(BRANCHES):
        r = i % d
        n = i // d
        o, m, l = _attn_block(n > 0, window // d, *in_refs[5 * bi:5 * bi + 5])
        start = (local // d) * (d * ATTN_BLOCK) + r
        rows = pl.ds(start, ATTN_BLOCK) if d == 1 else pl.ds(start, ATTN_BLOCK, stride=d)
        for g in range(ATTN_WIDTH // LANES):
            acc_o[bi][g, rows, :] = o[g]
            acc_m[bi][g, rows, :] = m[g]
            acc_l[bi][g, rows, :] = l[g]

    @pl.when(local == group_steps - 1)
    def _():
        total = lambda xs: functools.reduce(lambda a, b_: a + b_, xs)

        def merge(c, carry):
            rows = pl.ds(pl.multiple_of(c * MERGE_ROWS, MERGE_ROWS), MERGE_ROWS)
            for g in range(ATTN_WIDTH // LANES):
                ms = [a[g, rows, :] for a in acc_m]
                top = functools.reduce(jnp.maximum, ms)
                es = [jnp.exp2(m - top) for m in ms]
                den = total([e * a[g, rows, :] for e, a in zip(es, acc_l)])
                num = total([e * a[g, rows, :] for e, a in zip(es, acc_o)])
                out_ref[0, rows, g * LANES:(g + 1) * LANES] = (num / den).astype(BF16)
            return carry

        lax.fori_loop(0, ATTN_GROUP // MERGE_ROWS, merge, 0)


def _attention(qkv):
    b = qkv[0].shape[0]
    s = qkv[0].shape[1] * DILATIONS[0]
    steps = s // ATTN_BLOCK
    group_steps = ATTN_GROUP // ATTN_BLOCK
    in_specs = []
    for d in DILATIONS:
        cur = lambda bi, i, d=d: (bi, i // d, i % d)
        prev = lambda bi, i, d=d: (bi, jnp.maximum(i // d - 1, 0), i % d)
        wide = lambda m: pl.BlockSpec((1, ATTN_BLOCK, ATTN_WIDTH), m)
        in_specs += [wide(cur), wide(prev), wide(cur), wide(prev), wide(cur)]
    args = []
    for bi in range(len(DILATIONS)):
        q, k, v = qkv[3 * bi:3 * bi + 3]
        args += [q, k, k, v, v]
    nbr = len(BRANCHES)
    return pl.pallas_call(
        functools.partial(_attn_kernel, steps),
        grid=(b, steps),
        in_specs=in_specs,
        out_specs=pl.BlockSpec((1, ATTN_GROUP, ATTN_WIDTH), lambda bi, i: (bi, i // group_steps, 0)),
        out_shape=jax.ShapeDtypeStruct((b, s, ATTN_WIDTH), BF16),
        scratch_shapes=[pltpu.VMEM((ATTN_WIDTH // LANES, ATTN_GROUP, LANES), F32)] * (3 * nbr),
        compiler_params=pltpu.CompilerParams(
            dimension_semantics=("arbitrary", "arbitrary"), vmem_limit_bytes=VMEM_LIMIT),
        name="dilated_attention",
    )(*args)


def _tail_kernel(x_ref, attn_ref, y_ref, z_ref, p_ref, gssd_ref, gmix_ref, wout_ref, gpre_ref,
                 wup_ref, wdown_ref, gmlp_ref, wgate_ref, wproj_ref, gple_ref, out_ref):
    zf = z_ref[...].astype(F32)
    y = (_rms(y_ref[...].astype(F32) * (zf * _sigmoid(zf))) * gssd_ref[...]).astype(BF16)
    mix = _dot(attn_ref[...], wout_ref[0:ATTN_WIDTH, :]) + _dot(y, wout_ref[ATTN_WIDTH:, :])
    h = x_ref[...] + _rms(mix) * gmix_ref[...]

    u = (_rms(h) * gpre_ref[...]).astype(BF16)
    ff = jnp.zeros(h.shape, F32)
    for c in range(D_FF // FF_CHUNK):
        sl = slice(c * FF_CHUNK, (c + 1) * FF_CHUNK)
        hid = jnp.square(jnp.maximum(_dot(u, wup_ref[:, sl]), 0.0)).astype(BF16)
        ff = ff + _dot(hid, wdown_ref[sl, :])
    h = h + _rms(ff) * gmlp_ref[...]

    gate = _sigmoid(_dot(h.astype(BF16), wgate_ref[...]))
    ple = _dot(p_ref[...].astype(BF16), wproj_ref[...]) * gate
    out_ref[...] = h + _rms(ple) * gple_ref[...]


def _tail(x2, attn2, y2, z2, p2, gssd, gmix, wout, gpre, wup, wdown, gmlp, wgate, wproj, gple):
    n = x2.shape[0]
    t = TAIL_TILE
    tile = lambda w_: pl.BlockSpec((t, w_), lambda i: (i, 0))
    full = lambda a: pl.BlockSpec(a.shape, lambda i: (0, 0), pipeline_mode=pl.Buffered(1))
    return pl.pallas_call(
        _tail_kernel,
        grid=(n // t,),
        in_specs=[tile(D_MODEL), tile(ATTN_WIDTH), tile(SSD_WIDTH), tile(SSD_WIDTH),
                  tile(PLE_DIM), full(gssd), full(gmix),
                  full(wout), full(gpre), full(wup), full(wdown), full(gmlp), full(wgate),
                  full(wproj), full(gple)],
        out_specs=tile(D_MODEL),
        out_shape=jax.ShapeDtypeStruct((n, D_MODEL), F32),
        compiler_params=pltpu.CompilerParams(
            dimension_semantics=("arbitrary",), vmem_limit_bytes=VMEM_LIMIT),
        name="tail",
    )(x2, attn2, y2, z2, p2, gssd, gmix, wout, gpre, wup, wdown, gmlp, wgate, wproj, gple)


def _pad_lanes(a):
    return jnp.pad(a, ((0, 0), (0, LANES - a.shape[-1])))


def kernel(x, p, positions, norm_mix_pre, norm_mix_post, w_in, conv_w, conv_b, dt_bias, a_log,
           d_skip, ssd_norm_g, w_out, norm_mlp_pre, norm_mlp_post, w_up, w_down, w_ple_gate,
           w_ple_proj, norm_ple_post):
    b, s, _ = x.shape
    half = HEAD_DIM // 2
    invf = (ROPE_THETA ** (-jnp.arange(half, dtype=F32) * 2.0 / HEAD_DIM)).reshape(half, 1)
    pos3 = positions.reshape(b, 1, s)
    row = lambda a: a.reshape(1, -1)

    h = x
    for i in range(w_in.shape[0]):
        wi = w_in[i].astype(BF16)
        c0 = 3 * ATTN_WIDTH
        c1 = c0 + SSD_WIDTH
        c2 = c1 + CONV_CH
        *qkv, z, y = _in_proj(
            h, pos3, invf, row(norm_mix_pre[i]),
            wi[:, 0:ATTN_WIDTH], wi[:, ATTN_WIDTH:2 * ATTN_WIDTH], wi[:, 2 * ATTN_WIDTH:c0],
            wi[:, c0:c1], wi[:, c1:c2], _pad_lanes(wi[:, c2:]),
            conv_w[i], row(conv_b[i]), _pad_lanes(row(dt_bias[i])), _pad_lanes(row(a_log[i])),
            jnp.pad(a_log[i], (0, SSD_HEADS)).reshape(2 * SSD_HEADS, 1),
            row(jnp.repeat(d_skip[i], SSD_HEAD_DIM)))
        attn = _attention(qkv)
        h = _tail(h.reshape(b * s, D_MODEL), attn.reshape(b * s, ATTN_WIDTH),
                  y.reshape(b * s, SSD_WIDTH), z.reshape(b * s, SSD_WIDTH),
                  p[i].reshape(b * s, PLE_DIM), row(ssd_norm_g[i]),
                  row(norm_mix_post[i]), w_out[i].astype(BF16), row(norm_mlp_pre[i]),
                  w_up[i].astype(BF16), w_down[i].astype(BF16), row(norm_mlp_post[i]),
                  w_ple_gate[i].astype(BF16), w_ple_proj[i].astype(BF16),
                  row(norm_ple_post[i])).reshape(b, s, D_MODEL)
    return h
```

```python
import functools
import math

import jax
import jax.numpy as jnp
from jax import lax
from jax.experimental import pallas as pl
from jax.experimental.pallas import tpu as pltpu

D_MODEL = 1024
PLE_DIM = 256
N_HEADS = 8
HEAD_DIM = 64
ATTN_WIDTH = N_HEADS * HEAD_DIM
SSD_HEADS = 8
SSD_HEAD_DIM = 64
SSD_WIDTH = SSD_HEADS * SSD_HEAD_DIM
SSD_STATE = 128
CONV_WIDTH = 4
CONV_CH = SSD_WIDTH + 2 * SSD_STATE
CHUNK = 128
D_FF = 4 * D_MODEL
ROPE_THETA = 10000.0
BRANCHES = ((128, 1), (512, 4), (2048, 16))
DILATIONS = tuple(d for _, d in BRANCHES)
ATTN_BLOCK = 128
NORM_EPS = 1e-6
Q_SCALE = HEAD_DIM ** -0.5 * math.log2(math.e)

LANES = 128
CONV_HALO = 8
PROJ_TILE = 512
TAIL_TILE = 512
FF_CHUNK = 1024
MERGE_ROWS = 256
ATTN_UNROLL = 4
ATTN_GROUP = ATTN_BLOCK * max(DILATIONS)
VMEM_LIMIT = 56 * 1024 * 1024

F32 = jnp.float32
BF16 = jnp.bfloat16


def _dot(a, b):
    return jnp.dot(a, b, preferred_element_type=F32)


def _dot_nt(a, b):
    return lax.dot_general(a, b, (((1,), (1,)), ((), ())), preferred_element_type=F32)


def _sigmoid(x):
    return 1.0 / (1.0 + jnp.exp(-x))


def _rms(x):
    return x * lax.rsqrt(jnp.mean(x * x, axis=-1, keepdims=True) + NORM_EPS)


def _split3(a):
    hi = a.astype(BF16)
    rest = a - hi.astype(F32)
    mid = rest.astype(BF16)
    return hi, mid, (rest - mid.astype(F32)).astype(BF16)


def _ssd_prepare(bm, cm, dt, a_row, a_col, tril):
    cs = sum(_dot(tril, piece) for piece in _split3(dt * a_row))
    dt_t = dt.T[0:2 * SSD_HEADS]
    cs_t = sum(_dot_nt(piece, tril) for piece in _split3(dt_t * a_col))
    w_t = dt_t * jnp.exp2(cs_t[:, CHUNK - 1:CHUNK] - cs_t)
    return dict(cs=cs, cs_dt_t=cs_t - jnp.log2(dt_t), w_t=w_t, cb=_dot_nt(cm, bm),
                cm_f=cm.astype(F32), bm_t=bm.astype(F32).T)


def _ssd_chunk(xs_b, pre, dskip, prev, causal, low):
    zero = jnp.zeros((CHUNK, LANES), BF16)
    ys, nxt = [], []
    for g in range(SSD_WIDTH // LANES):
        x2 = xs_b[:, g * LANES:(g + 1) * LANES]
        rhs = jnp.concatenate([x2, prev[g].astype(BF16)], axis=0)
        y_lhs, inj_lhs, dec2 = [], [], []
        for h in (2 * g, 2 * g + 1):
            cs_h = jnp.broadcast_to(pre["cs"][:, h:h + 1], (CHUNK, CHUNK))
            grow = jnp.exp2(cs_h)
            seg = jnp.exp2(jnp.where(causal, cs_h - pre["cs_dt_t"][h:h + 1, :], -jnp.inf))
            y_lhs.append(jnp.concatenate([pre["cb"] * seg, pre["cm_f"] * grow],
                                         axis=1).astype(BF16))
            inj_lhs.append((pre["bm_t"] * pre["w_t"][h:h + 1, :]).astype(BF16))
            dec2.append(grow[CHUNK - 1:CHUNK, :])
        y_pair = _dot(jnp.concatenate(y_lhs, axis=0), rhs)
        ys.append(jnp.where(low, y_pair[:CHUNK], y_pair[CHUNK:]))
        x_split = jnp.concatenate([jnp.where(low, x2, zero), jnp.where(low, zero, x2)], axis=0)
        inject = _dot(jnp.concatenate(inj_lhs, axis=1), x_split)
        nxt.append(prev[g] * jnp.where(low[0:1], dec2[0], dec2[1]) + inject)

    return (jnp.concatenate(ys, axis=1) + dskip * xs_b.astype(F32)).astype(BF16), nxt


def _in_proj_kernel(x_ref, pos_ref, invf_ref, g_ref, wq_ref, wk_ref, wv_ref, wz_ref,
                    wxbc_ref, wdt_ref, convw_ref, convb_ref, dtb_ref, alog_ref, alogc_ref,
                    dskip_ref, *rest):
    n_qkv = 3 * len(DILATIONS)
    qkv_refs = rest[:n_qkv]
    z_ref, y_ref, cbuf, qs, ks, vs = rest[n_qkv:n_qkv + 6]
    stages, state = rest[n_qkv + 6:-1], rest[-1]
    t = x_ref.shape[1]
    j = pl.program_id(1)
    u = (_rms(x_ref[0]) * g_ref[...]).astype(BF16)

    ang = invf_ref[...] * pos_ref[0].astype(F32)
    cos4 = jnp.concatenate([jnp.cos(ang)] * 4, axis=0).T
    sin4 = jnp.concatenate([jnp.sin(ang)] * 4, axis=0).T
    lane = lax.broadcasted_iota(jnp.int32, (t, LANES), 1)
    first_half = (lane % HEAD_DIM) < (HEAD_DIM // 2)
    sin_signed = jnp.where(first_half, -sin4, sin4)

    def rope(v):
        partner = jnp.where(first_half, pltpu.roll(v, LANES - HEAD_DIM // 2, 1),
                            pltpu.roll(v, HEAD_DIM // 2, 1))
        return v * cos4 + partner * sin_signed

    q = _dot(u, wq_ref[...])
    k = _dot(u, wk_ref[...])
    v = _dot(u, wv_ref[...])
    for g in range(ATTN_WIDTH // LANES):
        sl = slice(g * LANES, (g + 1) * LANES)
        qs[g] = rope(q[:, sl]) * Q_SCALE
        ks[g] = rope(k[:, sl])
        vs[g] = v[:, sl]

    planes, d_prev = (qs, ks, vs), 1
    for bi, d in enumerate(DILATIONS):
        f = d // d_prev
        staged = stages[3 * (bi - 1):3 * bi] if 0 < bi < len(DILATIONS) - 1 else (None,) * 3
        for src, stage, dst in zip(planes, staged, qkv_refs[3 * bi:3 * bi + 3]):
            for r in range(d):
                start = (r % d_prev) * (t // d_prev) + r // d_prev
                for g in range(ATTN_WIDTH // LANES):
                    val = src[g] if d == 1 else src[g, pl.ds(start, t // d, stride=f), :]
                    if stage is not None:
                        stage[g, r * (t // d):(r + 1) * (t // d), :] = val
                    c = r * ATTN_WIDTH + g * LANES
                    dst[0, :, c:c + LANES] = val.astype(BF16)
        if staged[0] is not None:
            planes, d_prev = staged, d

    z_ref[0] = _dot(u, wz_ref[...]).astype(BF16)

    xbc = _dot(u, wxbc_ref[...])

    @pl.when(j == 0)
    def _():
        cbuf[0:CONV_HALO, :] = jnp.zeros((CONV_HALO, CONV_CH), F32)
        state[...] = jnp.zeros(state.shape, F32)

    cbuf[CONV_HALO:CONV_HALO + t, :] = xbc
    w = convw_ref[...]
    y = convb_ref[...] + w[CONV_WIDTH - 1:CONV_WIDTH] * xbc
    for s in range(1, CONV_WIDTH):
        y = y + w[CONV_WIDTH - 1 - s:CONV_WIDTH - s] * cbuf[CONV_HALO - s:CONV_HALO - s + t, :]
    cbuf[0:CONV_HALO, :] = cbuf[t:t + CONV_HALO, :]
    act = y * _sigmoid(y)
    xs_b = act[:, :SSD_WIDTH].astype(BF16)
    bm = act[:, SSD_WIDTH:SSD_WIDTH + SSD_STATE].astype(BF16)
    cm = act[:, SSD_WIDTH + SSD_STATE:].astype(BF16)

    dtr = _dot(u, wdt_ref[...]) + dtb_ref[...]
    dt = jnp.maximum(dtr, 0.0) + jnp.log1p(jnp.exp(-jnp.abs(dtr)))

    row = lax.broadcasted_iota(jnp.int32, (CHUNK, CHUNK), 0)
    col = lax.broadcasted_iota(jnp.int32, (CHUNK, CHUNK), 1)
    causal = row >= col
    low = col < SSD_HEAD_DIM
    tril = causal.astype(BF16)
    a_row = -jnp.exp(alog_ref[...]) * math.log2(math.e)
    a_col = -jnp.exp(alogc_ref[...]) * math.log2(math.e)
    groups = SSD_WIDTH // LANES
    st = [state[:, g * LANES:(g + 1) * LANES] for g in range(groups)]
    for c in range(t // CHUNK):
        rows = slice(c * CHUNK, (c + 1) * CHUNK)
        pre_c = _ssd_prepare(bm[rows], cm[rows], dt[rows], a_row, a_col, tril)
        yc, st = _ssd_chunk(xs_b[rows], pre_c, dskip_ref[...], st, causal, low)
        y_ref[0, rows, :] = yc
    for g in range(groups):
        state[:, g * LANES:(g + 1) * LANES] = st[g]


def _in_proj(x, pos3, invf, g, wq, wk, wv, wz, wxbc, wdt, convw, convb, dtb, alog, alogc, dskip):
    b, s, _ = x.shape
    t = PROJ_TILE
    tile = lambda w_: pl.BlockSpec((1, t, w_), lambda bi, j: (bi, j, 0))
    full = lambda a: pl.BlockSpec(a.shape, lambda bi, j: (0,) * a.ndim)
    wide = jax.ShapeDtypeStruct((b, s, ATTN_WIDTH), BF16)
    qkv_specs, qkv_shapes = [], []
    for d in DILATIONS:
        qkv_specs += [pl.BlockSpec((1, t // d, d * ATTN_WIDTH), lambda bi, j: (bi, j, 0))] * 3
        qkv_shapes += [jax.ShapeDtypeStruct((b, s // d, d * ATTN_WIDTH), BF16)] * 3
    plane = pltpu.VMEM((ATTN_WIDTH // LANES, t, LANES), F32)
    return pl.pallas_call(
        _in_proj_kernel,
        grid=(b, s // t),
        in_specs=[tile(D_MODEL), pl.BlockSpec((1, 1, t), lambda bi, j: (bi, 0, j)),
                  full(invf), full(g), full(wq), full(wk), full(wv), full(wz), full(wxbc),
                  full(wdt), full(convw), full(convb), full(dtb), full(alog), full(alogc),
                  full(dskip)],
        out_specs=qkv_specs + [tile(ATTN_WIDTH), tile(SSD_WIDTH)],
        out_shape=qkv_shapes + [wide, wide],
        scratch_shapes=[pltpu.VMEM((t + CONV_HALO, CONV_CH), F32)]
                       + [plane] * (3 * (len(DILATIONS) - 1))
                       + [pltpu.VMEM((SSD_STATE, SSD_WIDTH), F32)],
        compiler_params=pltpu.CompilerParams(
            dimension_semantics=("arbitrary", "arbitrary"), vmem_limit_bytes=VMEM_LIMIT),
        name="in_proj",
    )(x, pos3, invf, g, wq, wk, wv, wz, wxbc, wdt, convw, convb, dtb, alog, alogc, dskip)


def _attn_block(has_prev, span, q, kp, kc, vp, vc):
    blk = ATTN_BLOCK
    qi = lax.broadcasted_iota(jnp.int32, (blk, 2 * blk), 0)
    ki = lax.broadcasted_iota(jnp.int32, (blk, 2 * blk), 1)
    dist = qi + blk - ki
    valid = (dist >= 0) & (dist <= span) & ((ki >= blk) | has_prev)
    bias = jnp.where(valid, 0.0, -jnp.inf).astype(F32)
    bias = jnp.concatenate([bias, bias], axis=0)
    low = lax.broadcasted_iota(jnp.int32, (blk, LANES), 1) < HEAD_DIM

    k = jnp.concatenate([kp, kc], axis=0)
    v = jnp.concatenate([vp, vc], axis=0)
    outs, maxes, sums = [], [], []
    for g in range(ATTN_WIDTH // LANES):
        sl = slice(g * LANES, (g + 1) * LANES)
        q2, k2, v2 = q[:, sl], k[:, sl], v[:, sl]
        zero = jnp.zeros_like(q2)
        qq = jnp.concatenate([jnp.where(low, q2, zero), jnp.where(low, zero, q2)], axis=0)
        s = _dot_nt(qq, k2) + bias
        m = jnp.max(s, axis=-1, keepdims=True)
        p = jnp.exp2(s - m)
        l = jnp.sum(p, axis=-1, keepdims=True)
        o = _dot(p.astype(BF16), v2)
        outs.append(jnp.where(low, o[:blk], o[blk:]))
        maxes.append(jnp.where(low, m[:blk], m[blk:]))
        sums.append(jnp.where(low, l[:blk], l[blk:]))
    return outs, maxes, sums


def _attn_kernel(*refs):
    nbr = len(BRANCHES)
    in_refs = refs[:5 * nbr]
    out_ref = refs[5 * nbr]
    acc_o = refs[5 * nbr + 1:5 * nbr + 1 + nbr]
    acc_m = refs[5 * nbr + 1 + nbr:5 * nbr + 1 + 2 * nbr]
    acc_l = refs[5 * nbr + 1 + 2 * nbr:]
    group_steps = ATTN_GROUP // ATTN_BLOCK
    blk, wid = ATTN_BLOCK, ATTN_WIDTH

    for sub in range(ATTN_UNROLL):
        i = pl.program_id(1) * ATTN_UNROLL + sub
        local = i % group_steps
        for bi, (window, d) in enumerate(BRANCHES):
            q_ref, kp_ref, kc_ref, vp_ref, vc_ref = in_refs[5 * bi:5 * bi + 5]
            if d == 1:
                rows = slice(sub * blk, (sub + 1) * blk)
                before = slice((sub - 1) * blk, sub * blk)
                ops = (q_ref[0, rows, :],
                       kp_ref[0] if sub == 0 else kc_ref[0, before, :], kc_ref[0, rows, :],
                       vp_ref[0] if sub == 0 else vc_ref[0, before, :], vc_ref[0, rows, :])
                has_prev = (i > 0) if sub == 0 else True
            else:
                cols = slice(sub * wid, (sub + 1) * wid)
                ops = tuple(ref[0, :, cols] for ref in (q_ref, kp_ref, kc_ref, vp_ref, vc_ref))
                has_prev = (i // d) > 0
            o, m, l = _attn_block(has_prev, window // d, *ops)
            start = (local // d) * (d * blk) + i % d
            rows = pl.ds(start, blk) if d == 1 else pl.ds(start, blk, stride=d)
            for g in range(wid // LANES):
                acc_o[bi][g, rows, :] = o[g]
                acc_m[bi][g, rows, :] = m[g]
                acc_l[bi][g, rows, :] = l[g]

    @pl.when(local == group_steps - 1)
    def _():
        total = lambda xs: functools.reduce(lambda a, b_: a + b_, xs)

        def merge(c, carry):
            rows = pl.ds(pl.multiple_of(c * MERGE_ROWS, MERGE_ROWS), MERGE_ROWS)
            for g in range(ATTN_WIDTH // LANES):
                ms = [a[g, rows, :] for a in acc_m]
                top = functools.reduce(jnp.maximum, ms)
                es = [jnp.exp2(m - top) for m in ms]
                den = total([e * a[g, rows, :] for e, a in zip(es, acc_l)])
                num = total([e * a[g, rows, :] for e, a in zip(es, acc_o)])
                out_ref[0, rows, g * LANES:(g + 1) * LANES] = (num / den).astype(BF16)
            return carry

        lax.fori_loop(0, ATTN_GROUP // MERGE_ROWS, merge, 0)


def _attention(qkv):
    b = qkv[0].shape[0]
    s = qkv[0].shape[1] * DILATIONS[0]
    un = ATTN_UNROLL
    steps = s // (ATTN_BLOCK * un)
    group_steps = ATTN_GROUP // (ATTN_BLOCK * un)
    in_specs = []
    for d in DILATIONS:
        if d == 1:
            cur = pl.BlockSpec((1, un * ATTN_BLOCK, ATTN_WIDTH), lambda bi, j: (bi, j, 0))
            prev = pl.BlockSpec((1, ATTN_BLOCK, ATTN_WIDTH),
                                lambda bi, j: (bi, jnp.maximum(j * un - 1, 0), 0))
        else:
            assert d % un == 0
            cur = pl.BlockSpec((1, ATTN_BLOCK, un * ATTN_WIDTH),
                               lambda bi, j, d=d: (bi, (j * un) // d, (j * un % d) // un))
            prev = pl.BlockSpec((1, ATTN_BLOCK, un * ATTN_WIDTH),
                                lambda bi, j, d=d: (bi, jnp.maximum((j * un) // d - 1, 0),
                                                    (j * un % d) // un))
        in_specs += [cur, prev, cur, prev, cur]
    args = []
    for bi in range(len(DILATIONS)):
        q, k, v = qkv[3 * bi:3 * bi + 3]
        args += [q, k, k, v, v]
    nbr = len(BRANCHES)
    return pl.pallas_call(
        _attn_kernel,
        grid=(b, steps),
        in_specs=in_specs,
        out_specs=pl.BlockSpec((1, ATTN_GROUP, ATTN_WIDTH), lambda bi, i: (bi, i // group_steps, 0)),
        out_shape=jax.ShapeDtypeStruct((b, s, ATTN_WIDTH), BF16),
        scratch_shapes=[pltpu.VMEM((ATTN_WIDTH // LANES, ATTN_GROUP, LANES), F32)] * (3 * nbr),
        compiler_params=pltpu.CompilerParams(
            dimension_semantics=("arbitrary", "arbitrary"), vmem_limit_bytes=VMEM_LIMIT),
        name="dilated_attention",
    )(*args)


def _tail_kernel(x_ref, attn_ref, y_ref, z_ref, p_ref, gssd_ref, gmix_ref, wout_ref, gpre_ref,
                 wup_ref, wdown_ref, gmlp_ref, wgate_ref, wproj_ref, gple_ref, out_ref):
    zf = z_ref[...].astype(F32)
    y = (_rms(y_ref[...].astype(F32) * (zf * _sigmoid(zf))) * gssd_ref[...]).astype(BF16)
    mix = _dot(attn_ref[...], wout_ref[0:ATTN_WIDTH, :]) + _dot(y, wout_ref[ATTN_WIDTH:, :])
    h = x_ref[...] + _rms(mix) * gmix_ref[...]

    u = (_rms(h) * gpre_ref[...]).astype(BF16)
    ff = jnp.zeros(h.shape, F32)
    for c in range(D_FF // FF_CHUNK):
        sl = slice(c * FF_CHUNK, (c + 1) * FF_CHUNK)
        hid = jnp.square(jnp.maximum(_dot(u, wup_ref[:, sl]), 0.0)).astype(BF16)
        ff = ff + _dot(hid, wdown_ref[sl, :])
    h = h + _rms(ff) * gmlp_ref[...]

    gate = _sigmoid(_dot(h.astype(BF16), wgate_ref[...]))
    ple = _dot(p_ref[...].astype(BF16), wproj_ref[...]) * gate
    out_ref[...] = h + _rms(ple) * gple_ref[...]


def _tail(x2, attn2, y2, z2, p2, gssd, gmix, wout, gpre, wup, wdown, gmlp, wgate, wproj, gple):
    n = x2.shape[0]
    t = TAIL_TILE
    tile = lambda w_: pl.BlockSpec((t, w_), lambda i: (i, 0))
    full = lambda a: pl.BlockSpec(a.shape, lambda i: (0, 0), pipeline_mode=pl.Buffered(1))
    return pl.pallas_call(
        _tail_kernel,
        grid=(n // t,),
        in_specs=[tile(D_MODEL), tile(ATTN_WIDTH), tile(SSD_WIDTH), tile(SSD_WIDTH),
                  tile(PLE_DIM), full(gssd), full(gmix),
                  full(wout), full(gpre), full(wup), full(wdown), full(gmlp), full(wgate),
                  full(wproj), full(gple)],
        out_specs=tile(D_MODEL),
        out_shape=jax.ShapeDtypeStruct((n, D_MODEL), F32),
        compiler_params=pltpu.CompilerParams(
            dimension_semantics=("arbitrary",), vmem_limit_bytes=VMEM_LIMIT),
        name="tail",
    )(x2, attn2, y2, z2, p2, gssd, gmix, wout, gpre, wup, wdown, gmlp, wgate, wproj, gple)


def _pad_lanes(a):
    return jnp.pad(a, ((0, 0), (0, LANES - a.shape[-1])))


def kernel(x, p, positions, norm_mix_pre, norm_mix_post, w_in, conv_w, conv_b, dt_bias, a_log,
           d_skip, ssd_norm_g, w_out, norm_mlp_pre, norm_mlp_post, w_up, w_down, w_ple_gate,
           w_ple_proj, norm_ple_post):
    b, s, _ = x.shape
    half = HEAD_DIM // 2
    invf = (ROPE_THETA ** (-jnp.arange(half, dtype=F32) * 2.0 / HEAD_DIM)).reshape(half, 1)
    pos3 = positions.reshape(b, 1, s)
    row = lambda a: a.reshape(1, -1)

    h = x
    for i in range(w_in.shape[0]):
        wi = w_in[i].astype(BF16)
        c0 = 3 * ATTN_WIDTH
        c1 = c0 + SSD_WIDTH
        c2 = c1 + CONV_CH
        *qkv, z, y = _in_proj(
            h, pos3, invf, row(norm_mix_pre[i]),
            wi[:, 0:ATTN_WIDTH], wi[:, ATTN_WIDTH:2 * ATTN_WIDTH], wi[:, 2 * ATTN_WIDTH:c0],
            wi[:, c0:c1], wi[:, c1:c2], _pad_lanes(wi[:, c2:]),
            conv_w[i], row(conv_b[i]), _pad_lanes(row(dt_bias[i])), _pad_lanes(row(a_log[i])),
            jnp.pad(a_log[i], (0, SSD_HEADS)).reshape(2 * SSD_HEADS, 1),
            row(jnp.repeat(d_skip[i], SSD_HEAD_DIM)))
        attn = _attention(qkv)
        h = _tail(h.reshape(b * s, D_MODEL), attn.reshape(b * s, ATTN_WIDTH),
                  y.reshape(b * s, SSD_WIDTH), z.reshape(b * s, SSD_WIDTH),
                  p[i].reshape(b * s, PLE_DIM), row(ssd_norm_g[i]),
                  row(norm_mix_post[i]), w_out[i].astype(BF16), row(norm_mlp_pre[i]),
                  w_up[i].astype(BF16), w_down[i].astype(BF16), row(norm_mlp_post[i]),
                  w_ple_gate[i].astype(BF16), w_ple_proj[i].astype(BF16),
                  row(norm_ple_post[i])).reshape(b, s, D_MODEL)
    return h
```

```python
import functools
import math

import jax
import jax.numpy as jnp
from jax import lax
from jax.experimental import pallas as pl
from jax.experimental.pallas import tpu as pltpu

D_MODEL = 1024
PLE_DIM = 256
N_HEADS = 8
HEAD_DIM = 64
ATTN_WIDTH = N_HEADS * HEAD_DIM
SSD_HEADS = 8
SSD_HEAD_DIM = 64
SSD_WIDTH = SSD_HEADS * SSD_HEAD_DIM
SSD_STATE = 128
CONV_WIDTH = 4
CONV_CH = SSD_WIDTH + 2 * SSD_STATE
CHUNK = 128
D_FF = 4 * D_MODEL
ROPE_THETA = 10000.0
BRANCHES = ((128, 1), (512, 4), (2048, 16))
DILATIONS = tuple(d for _, d in BRANCHES)
ATTN_BLOCK = 128
NORM_EPS = 1e-6
Q_SCALE = HEAD_DIM ** -0.5 * math.log2(math.e)

LANES = 128
CONV_HALO = 8
PROJ_TILE = 512
TAIL_TILE = 512
FF_CHUNK = 1024
MERGE_ROWS = 256
ATTN_UNROLL = 4
ATTN_GROUP = ATTN_BLOCK * max(DILATIONS)
VMEM_LIMIT = 56 * 1024 * 1024

F32 = jnp.float32
BF16 = jnp.bfloat16


def _dot(a, b):
    return jnp.dot(a, b, preferred_element_type=F32)


def _dot_nt(a, b):
    return lax.dot_general(a, b, (((1,), (1,)), ((), ())), preferred_element_type=F32)


def _sigmoid(x):
    return 1.0 / (1.0 + jnp.exp(-x))


def _rms(x):
    return x * lax.rsqrt(jnp.mean(x * x, axis=-1, keepdims=True) + NORM_EPS)


def _split3(a):
    hi = a.astype(BF16)
    rest = a - hi.astype(F32)
    mid = rest.astype(BF16)
    return hi, mid, (rest - mid.astype(F32)).astype(BF16)


def _ssd_prepare(bm, cm, dt, a_row, a_col, tril):
    cs = sum(_dot(tril, piece) for piece in _split3(dt * a_row))
    dt_t = dt.T[0:2 * SSD_HEADS]
    cs_t = sum(_dot_nt(piece, tril) for piece in _split3(dt_t * a_col))
    w_t = dt_t * jnp.exp2(cs_t[:, CHUNK - 1:CHUNK] - cs_t)
    return dict(cs=cs, cs_dt_t=cs_t - jnp.log2(dt_t), w_t=w_t, cb=_dot_nt(cm, bm),
                cm_f=cm.astype(F32), bm_t=bm.astype(F32).T)


def _ssd_chunk(xs_b, pre, dskip, prev, causal, low):
    zero = jnp.zeros((CHUNK, LANES), BF16)
    ys, nxt = [], []
    for g in range(SSD_WIDTH // LANES):
        x2 = xs_b[:, g * LANES:(g + 1) * LANES]
        rhs = jnp.concatenate([x2, prev[g].astype(BF16)], axis=0)
        y_lhs, inj_lhs, dec2 = [], [], []
        for h in (2 * g, 2 * g + 1):
            cs_h = jnp.broadcast_to(pre["cs"][:, h:h + 1], (CHUNK, CHUNK))
            grow = jnp.exp2(cs_h)
            seg = jnp.exp2(jnp.where(causal, cs_h - pre["cs_dt_t"][h:h + 1, :], -jnp.inf))
            y_lhs.append(jnp.concatenate([pre["cb"] * seg, pre["cm_f"] * grow],
                                         axis=1).astype(BF16))
            inj_lhs.append((pre["bm_t"] * pre["w_t"][h:h + 1, :]).astype(BF16))
            dec2.append(grow[CHUNK - 1:CHUNK, :])
        y_pair = _dot(jnp.concatenate(y_lhs, axis=0), rhs)
        ys.append(jnp.where(low, y_pair[:CHUNK], y_pair[CHUNK:]))
        x_split = jnp.concatenate([jnp.where(low, x2, zero), jnp.where(low, zero, x2)], axis=0)
        inject = _dot(jnp.concatenate(inj_lhs, axis=1), x_split)
        nxt.append(prev[g] * jnp.where(low[0:1], dec2[0], dec2[1]) + inject)

    return (jnp.concatenate(ys, axis=1) + dskip * xs_b.astype(F32)).astype(BF16), nxt


def _in_proj_kernel(x_ref, pos_ref, invf_ref, g_ref, wq_ref, wk_ref, wv_ref, wz_ref,
                    wxbc_ref, wdt_ref, convw_ref, convb_ref, dtb_ref, alog_ref, alogc_ref,
                    dskip_ref, *rest):
    n_qkv = 3 * len(DILATIONS)
    qkv_refs = rest[:n_qkv]
    z_ref, y_ref, cbuf, qs, ks, vs = rest[n_qkv:n_qkv + 6]
    stages, state = rest[n_qkv + 6:-1], rest[-1]
    t = x_ref.shape[1]
    j = pl.program_id(1)
    u = (_rms(x_ref[0]) * g_ref[...]).astype(BF16)

    ang = invf_ref[...] * pos_ref[0].astype(F32)
    cos4 = jnp.concatenate([jnp.cos(ang)] * 4, axis=0).T
    sin4 = jnp.concatenate([jnp.sin(ang)] * 4, axis=0).T
    lane = lax.broadcasted_iota(jnp.int32, (t, LANES), 1)
    first_half = (lane % HEAD_DIM) < (HEAD_DIM // 2)
    sin_signed = jnp.where(first_half, -sin4, sin4)

    def rope(v):
        partner = jnp.where(first_half, pltpu.roll(v, LANES - HEAD_DIM // 2, 1),
                            pltpu.roll(v, HEAD_DIM // 2, 1))
        return v * cos4 + partner * sin_signed

    def emit_layouts(ti):
        planes, d_prev = (qs, ks, vs), 1
        for bi, d in enumerate(DILATIONS):
            f = d // d_prev
            staged = stages[3 * (bi - 1):3 * bi] if 0 < bi < len(DILATIONS) - 1 else (None,) * 3
            src, stage, dst = planes[ti], staged[ti], qkv_refs[3 * bi + ti]
            for r in range(d):
                start = (r % d_prev) * (t // d_prev) + r // d_prev
                for g in range(ATTN_WIDTH // LANES):
                    val = src[g] if d == 1 else src[g, pl.ds(start, t // d, stride=f), :]
                    if stage is not None:
                        stage[g, r * (t // d):(r + 1) * (t // d), :] = val
                    c = r * ATTN_WIDTH + g * LANES
                    dst[0, :, c:c + LANES] = val.astype(BF16)
            if staged[0] is not None:
                planes, d_prev = staged, d

    xbc = _dot(u, wxbc_ref[...])
    dtr = _dot(u, wdt_ref[...]) + dtb_ref[...]

    @pl.when(j == 0)
    def _():
        cbuf[0:CONV_HALO, :] = jnp.zeros((CONV_HALO, CONV_CH), F32)
        state[...] = jnp.zeros(state.shape, F32)

    cbuf[CONV_HALO:CONV_HALO + t, :] = xbc
    w = convw_ref[...]
    y = convb_ref[...] + w[CONV_WIDTH - 1:CONV_WIDTH] * xbc
    for s in range(1, CONV_WIDTH):
        y = y + w[CONV_WIDTH - 1 - s:CONV_WIDTH - s] * cbuf[CONV_HALO - s:CONV_HALO - s + t, :]
    cbuf[0:CONV_HALO, :] = cbuf[t:t + CONV_HALO, :]
    act = y * _sigmoid(y)
    xs_b = act[:, :SSD_WIDTH].astype(BF16)
    bm = act[:, SSD_WIDTH:SSD_WIDTH + SSD_STATE].astype(BF16)
    cm = act[:, SSD_WIDTH + SSD_STATE:].astype(BF16)

    dt = jnp.maximum(dtr, 0.0) + jnp.log1p(jnp.exp(-jnp.abs(dtr)))

    row = lax.broadcasted_iota(jnp.int32, (CHUNK, CHUNK), 0)
    col = lax.broadcasted_iota(jnp.int32, (CHUNK, CHUNK), 1)
    causal = row >= col
    low = col < SSD_HEAD_DIM
    tril = causal.astype(BF16)
    a_row = -jnp.exp(alog_ref[...]) * math.log2(math.e)
    a_col = -jnp.exp(alogc_ref[...]) * math.log2(math.e)
    groups = SSD_WIDTH // LANES
    st = [state[:, g * LANES:(g + 1) * LANES] for g in range(groups)]

    def ssd_step(c, st):
        rows = slice(c * CHUNK, (c + 1) * CHUNK)
        pre_c = _ssd_prepare(bm[rows], cm[rows], dt[rows], a_row, a_col, tril)
        yc, st = _ssd_chunk(xs_b[rows], pre_c, dskip_ref[...], st, causal, low)
        y_ref[0, rows, :] = yc
        return st

    lanes = [slice(g * LANES, (g + 1) * LANES) for g in range(ATTN_WIDTH // LANES)]
    assert t // CHUNK == 4
    q = _dot(u, wq_ref[...])
    st = ssd_step(0, st)
    for g, sl in enumerate(lanes):
        qs[g] = rope(q[:, sl]) * Q_SCALE
    k = _dot(u, wk_ref[...])
    emit_layouts(0)
    st = ssd_step(1, st)
    for g, sl in enumerate(lanes):
        ks[g] = rope(k[:, sl])
    v = _dot(u, wv_ref[...])
    emit_layouts(1)
    st = ssd_step(2, st)
    for g, sl in enumerate(lanes):
        vs[g] = v[:, sl]
    z_ref[0] = _dot(u, wz_ref[...]).astype(BF16)
    emit_layouts(2)
    st = ssd_step(3, st)
    for g in range(groups):
        state[:, g * LANES:(g + 1) * LANES] = st[g]


def _in_proj(x, pos3, invf, g, wq, wk, wv, wz, wxbc, wdt, convw, convb, dtb, alog, alogc, dskip):
    b, s, _ = x.shape
    t = PROJ_TILE
    tile = lambda w_: pl.BlockSpec((1, t, w_), lambda bi, j: (bi, j, 0))
    full = lambda a: pl.BlockSpec(a.shape, lambda bi, j: (0,) * a.ndim)
    wide = jax.ShapeDtypeStruct((b, s, ATTN_WIDTH), BF16)
    qkv_specs, qkv_shapes = [], []
    for d in DILATIONS:
        qkv_specs += [pl.BlockSpec((1, t // d, d * ATTN_WIDTH), lambda bi, j: (bi, j, 0))] * 3
        qkv_shapes += [jax.ShapeDtypeStruct((b, s // d, d * ATTN_WIDTH), BF16)] * 3
    plane = pltpu.VMEM((ATTN_WIDTH // LANES, t, LANES), F32)
    return pl.pallas_call(
        _in_proj_kernel,
        grid=(b, s // t),
        in_specs=[tile(D_MODEL), pl.BlockSpec((1, 1, t), lambda bi, j: (bi, 0, j)),
                  full(invf), full(g), full(wq), full(wk), full(wv), full(wz), full(wxbc),
                  full(wdt), full(convw), full(convb), full(dtb), full(alog), full(alogc),
                  full(dskip)],
        out_specs=qkv_specs + [tile(ATTN_WIDTH), tile(SSD_WIDTH)],
        out_shape=qkv_shapes + [wide, wide],
        scratch_shapes=[pltpu.VMEM((t + CONV_HALO, CONV_CH), F32)]
                       + [plane] * (3 * (len(DILATIONS) - 1))
                       + [pltpu.VMEM((SSD_STATE, SSD_WIDTH), F32)],
        compiler_params=pltpu.CompilerParams(
            dimension_semantics=("arbitrary", "arbitrary"), vmem_limit_bytes=VMEM_LIMIT),
        name="in_proj",
    )(x, pos3, invf, g, wq, wk, wv, wz, wxbc, wdt, convw, convb, dtb, alog, alogc, dskip)


def _attn_block(has_prev, span, q, kp, kc, vp, vc):
    blk = ATTN_BLOCK
    qi = lax.broadcasted_iota(jnp.int32, (blk, 2 * blk), 0)
    ki = lax.broadcasted_iota(jnp.int32, (blk, 2 * blk), 1)
    dist = qi + blk - ki
    valid = (dist >= 0) & (dist <= span) & ((ki >= blk) | has_prev)
    bias = jnp.where(valid, 0.0, -jnp.inf).astype(F32)
    bias = jnp.concatenate([bias, bias], axis=0)
    low = lax.broadcasted_iota(jnp.int32, (blk, LANES), 1) < HEAD_DIM

    k = jnp.concatenate([kp, kc], axis=0)
    v = jnp.concatenate([vp, vc], axis=0)
    outs, maxes, sums = [], [], []
    for g in range(ATTN_WIDTH // LANES):
        sl = slice(g * LANES, (g + 1) * LANES)
        q2, k2, v2 = q[:, sl], k[:, sl], v[:, sl]
        zero = jnp.zeros_like(q2)
        qq = jnp.concatenate([jnp.where(low, q2, zero), jnp.where(low, zero, q2)], axis=0)
        s = _dot_nt(qq, k2) + bias
        m = jnp.max(s, axis=-1, keepdims=True)
        p = jnp.exp2(s - m)
        l = jnp.sum(p, axis=-1, keepdims=True)
        o = _dot(p.astype(BF16), v2)
        outs.append(jnp.where(low, o[:blk], o[blk:]))
        maxes.append(jnp.where(low, m[:blk], m[blk:]))
        sums.append(jnp.where(low, l[:blk], l[blk:]))
    return outs, maxes, sums


def _attn_kernel(*refs):
    nbr = len(BRANCHES)
    in_refs = refs[:5 * nbr]
    out_ref = refs[5 * nbr]
    acc_o = refs[5 * nbr + 1:5 * nbr + 1 + nbr]
    acc_m = refs[5 * nbr + 1 + nbr:5 * nbr + 1 + 2 * nbr]
    acc_l = refs[5 * nbr + 1 + 2 * nbr:]
    group_steps = ATTN_GROUP // ATTN_BLOCK
    blk, wid = ATTN_BLOCK, ATTN_WIDTH

    for sub in range(ATTN_UNROLL):
        i = pl.program_id(1) * ATTN_UNROLL + sub
        local = i % group_steps
        for bi, (window, d) in enumerate(BRANCHES):
            q_ref, kp_ref, kc_ref, vp_ref, vc_ref = in_refs[5 * bi:5 * bi + 5]
            if d == 1:
                rows = slice(sub * blk, (sub + 1) * blk)
                before = slice((sub - 1) * blk, sub * blk)
                ops = (q_ref[0, rows, :],
                       kp_ref[0] if sub == 0 else kc_ref[0, before, :], kc_ref[0, rows, :],
                       vp_ref[0] if sub == 0 else vc_ref[0, before, :], vc_ref[0, rows, :])
                has_prev = (i > 0) if sub == 0 else True
            else:
                cols = slice(sub * wid, (sub + 1) * wid)
                ops = tuple(ref[0, :, cols] for ref in (q_ref, kp_ref, kc_ref, vp_ref, vc_ref))
                has_prev = (i // d) > 0
            o, m, l = _attn_block(has_prev, window // d, *ops)
            start = (local // d) * (d * blk) + i % d
            rows = pl.ds(start, blk) if d == 1 else pl.ds(start, blk, stride=d)
            for g in range(wid // LANES):
                acc_o[bi][g, rows, :] = o[g]
                acc_m[bi][g, rows, :] = m[g]
                acc_l[bi][g, rows, :] = l[g]

    @pl.when(local == group_steps - 1)
    def _():
        total = lambda xs: functools.reduce(lambda a, b_: a + b_, xs)

        def merge(c, carry):
            rows = pl.ds(pl.multiple_of(c * MERGE_ROWS, MERGE_ROWS), MERGE_ROWS)
            for g in range(ATTN_WIDTH // LANES):
                ms = [a[g, rows, :] for a in acc_m]
                top = functools.reduce(jnp.maximum, ms)
                es = [jnp.exp2(m - top) for m in ms]
                den = total([e * a[g, rows, :] for e, a in zip(es, acc_l)])
                num = total([e * a[g, rows, :] for e, a in zip(es, acc_o)])
                out_ref[0, rows, g * LANES:(g + 1) * LANES] = (num / den).astype(BF16)
            return carry

        lax.fori_loop(0, ATTN_GROUP // MERGE_ROWS, merge, 0)


def _attention(qkv):
    b = qkv[0].shape[0]
    s = qkv[0].shape[1] * DILATIONS[0]
    un = ATTN_UNROLL
    steps = s // (ATTN_BLOCK * un)
    group_steps = ATTN_GROUP // (ATTN_BLOCK * un)
    in_specs = []
    for d in DILATIONS:
        if d == 1:
            cur = pl.BlockSpec((1, un * ATTN_BLOCK, ATTN_WIDTH), lambda bi, j: (bi, j, 0))
            prev = pl.BlockSpec((1, ATTN_BLOCK, ATTN_WIDTH),
                                lambda bi, j: (bi, jnp.maximum(j * un - 1, 0), 0))
        else:
            assert d % un == 0
            cur = pl.BlockSpec((1, ATTN_BLOCK, un * ATTN_WIDTH),
                               lambda bi, j, d=d: (bi, (j * un) // d, (j * un % d) // un))
            prev = pl.BlockSpec((1, ATTN_BLOCK, un * ATTN_WIDTH),
                                lambda bi, j, d=d: (bi, jnp.maximum((j * un) // d - 1, 0),
                                                    (j * un % d) // un))
        in_specs += [cur, prev, cur, prev, cur]
    args = []
    for bi in range(len(DILATIONS)):
        q, k, v = qkv[3 * bi:3 * bi + 3]
        args += [q, k, k, v, v]
    nbr = len(BRANCHES)
    return pl.pallas_call(
        _attn_kernel,
        grid=(b, steps),
        in_specs=in_specs,
        out_specs=pl.BlockSpec((1, ATTN_GROUP, ATTN_WIDTH), lambda bi, i: (bi, i // group_steps, 0)),
        out_shape=jax.ShapeDtypeStruct((b, s, ATTN_WIDTH), BF16),
        scratch_shapes=[pltpu.VMEM((ATTN_WIDTH // LANES, ATTN_GROUP, LANES), F32)] * (3 * nbr),
        compiler_params=pltpu.CompilerParams(
            dimension_semantics=("arbitrary", "arbitrary"), vmem_limit_bytes=VMEM_LIMIT),
        name="dilated_attention",
    )(*args)


def _tail_kernel(x_ref, attn_ref, y_ref, z_ref, p_ref, gssd_ref, gmix_ref, wout_ref, gpre_ref,
                 wup_ref, wdown_ref, gmlp_ref, wgate_ref, wproj_ref, gple_ref, out_ref):
    def gate_ssd(d):
        zf = z_ref[d["rows"], :].astype(F32)
        y = y_ref[d["rows"], :].astype(F32) * (zf * _sigmoid(zf))
        d["y"] = (_rms(y) * gssd_ref[...]).astype(BF16)

    def out_proj(d):
        d["mix"] = (_dot(attn_ref[d["rows"], :], wout_ref[0:ATTN_WIDTH, :])
                    + _dot(d["y"], wout_ref[ATTN_WIDTH:, :]))

    def norm_mix(d):
        d["h"] = x_ref[d["rows"], :] + _rms(d["mix"]) * gmix_ref[...]
        d["u"] = (_rms(d["h"]) * gpre_ref[...]).astype(BF16)

    def mlp(d):
        ff = jnp.zeros(d["h"].shape, F32)
        for c in range(D_FF // FF_CHUNK):
            sl = slice(c * FF_CHUNK, (c + 1) * FF_CHUNK)
            hid = jnp.square(jnp.maximum(_dot(d["u"], wup_ref[:, sl]), 0.0)).astype(BF16)
            ff = ff + _dot(hid, wdown_ref[sl, :])
        d["ff"] = ff

    def norm_mlp(d):
        d["h"] = d["h"] + _rms(d["ff"]) * gmlp_ref[...]

    def ple_proj(d):
        d["gate"] = _dot(d["h"].astype(BF16), wgate_ref[...])
        d["ple"] = _dot(p_ref[d["rows"], :].astype(BF16), wproj_ref[...])

    def norm_ple(d):
        ple = d["ple"] * _sigmoid(d["gate"])
        out_ref[d["rows"], :] = d["h"] + _rms(ple) * gple_ref[...]

    half = x_ref.shape[0] // 2
    a = dict(rows=slice(0, half))
    b = dict(rows=slice(half, 2 * half))
    for stage, d in ((gate_ssd, a), (out_proj, a), (gate_ssd, b), (norm_mix, a), (out_proj, b),
                     (mlp, a), (norm_mix, b), (mlp, b), (norm_mlp, a), (ple_proj, a),
                     (norm_mlp, b), (norm_ple, a), (ple_proj, b), (norm_ple, b)):
        stage(d)


def _tail(x2, attn2, y2, z2, p2, gssd, gmix, wout, gpre, wup, wdown, gmlp, wgate, wproj, gple):
    n = x2.shape[0]
    t = TAIL_TILE
    tile = lambda w_: pl.BlockSpec((t, w_), lambda i: (i, 0))
    full = lambda a: pl.BlockSpec(a.shape, lambda i: (0, 0), pipeline_mode=pl.Buffered(1))
    return pl.pallas_call(
        _tail_kernel,
        grid=(n // t,),
        in_specs=[tile(D_MODEL), tile(ATTN_WIDTH), tile(SSD_WIDTH), tile(SSD_WIDTH),
                  tile(PLE_DIM), full(gssd), full(gmix),
                  full(wout), full(gpre), full(wup), full(wdown), full(gmlp), full(wgate),
                  full(wproj), full(gple)],
        out_specs=tile(D_MODEL),
        out_shape=jax.ShapeDtypeStruct((n, D_MODEL), F32),
        compiler_params=pltpu.CompilerParams(
            dimension_semantics=("arbitrary",), vmem_limit_bytes=VMEM_LIMIT),
        name="tail",
    )(x2, attn2, y2, z2, p2, gssd, gmix, wout, gpre, wup, wdown, gmlp, wgate, wproj, gple)


def _pad_lanes(a):
    return jnp.pad(a, ((0, 0), (0, LANES - a.shape[-1])))


def kernel(x, p, positions, norm_mix_pre, norm_mix_post, w_in, conv_w, conv_b, dt_bias, a_log,
           d_skip, ssd_norm_g, w_out, norm_mlp_pre, norm_mlp_post, w_up, w_down, w_ple_gate,
           w_ple_proj, norm_ple_post):
    b, s, _ = x.shape
    half = HEAD_DIM // 2
    invf = (ROPE_THETA ** (-jnp.arange(half, dtype=F32) * 2.0 / HEAD_DIM)).reshape(half, 1)
    pos3 = positions.reshape(b, 1, s)
    row = lambda a: a.reshape(1, -1)

    h = x
    for i in range(w_in.shape[0]):
        wi = w_in[i].astype(BF16)
        c0 = 3 * ATTN_WIDTH
        c1 = c0 + SSD_WIDTH
        c2 = c1 + CONV_CH
        *qkv, z, y = _in_proj(
            h, pos3, invf, row(norm_mix_pre[i]),
            wi[:, 0:ATTN_WIDTH], wi[:, ATTN_WIDTH:2 * ATTN_WIDTH], wi[:, 2 * ATTN_WIDTH:c0],
            wi[:, c0:c1], wi[:, c1:c2], _pad_lanes(wi[:, c2:]),
            conv_w[i], row(conv_b[i]), _pad_lanes(row(dt_bias[i])), _pad_lanes(row(a_log[i])),
            jnp.pad(a_log[i], (0, SSD_HEADS)).reshape(2 * SSD_HEADS, 1),
            row(jnp.repeat(d_skip[i], SSD_HEAD_DIM)))
        attn = _attention(qkv)
        h = _tail(h.reshape(b * s, D_MODEL), attn.reshape(b * s, ATTN_WIDTH),
                  y.reshape(b * s, SSD_WIDTH), z.reshape(b * s, SSD_WIDTH),
                  p[i].reshape(b * s, PLE_DIM), row(ssd_norm_g[i]),
                  row(norm_mix_post[i]), w_out[i].astype(BF16), row(norm_mlp_pre[i]),
                  w_up[i].astype(BF16), w_down[i].astype(BF16), row(norm_mlp_post[i]),
                  w_ple_gate[i].astype(BF16), w_ple_proj[i].astype(BF16),
                  row(norm_ple_post[i])).reshape(b, s, D_MODEL)
    return h
```

```python
import functools
import math

import jax
import jax.numpy as jnp
from jax import lax
from jax.experimental import pallas as pl
from jax.experimental.pallas import tpu as pltpu

D_MODEL = 1024
PLE_DIM = 256
N_HEADS = 8
HEAD_DIM = 64
ATTN_WIDTH = N_HEADS * HEAD_DIM
SSD_HEADS = 8
SSD_HEAD_DIM = 64
SSD_WIDTH = SSD_HEADS * SSD_HEAD_DIM
SSD_STATE = 128
CONV_WIDTH = 4
CONV_CH = SSD_WIDTH + 2 * SSD_STATE
CHUNK = 128
D_FF = 4 * D_MODEL
ROPE_THETA = 10000.0
BRANCHES = ((128, 1), (512, 4), (2048, 16))
DILATIONS = tuple(d for _, d in BRANCHES)
ATTN_BLOCK = 128
NORM_EPS = 1e-6
Q_SCALE = HEAD_DIM ** -0.5 * math.log2(math.e)

LANES = 128
CONV_HALO = 8
PROJ_TILE = 512
ROW_CHUNK = 32
TAIL_TILE = 512
FF_CHUNK = 1024
MERGE_ROWS = 256
ATTN_UNROLL = 4
ATTN_GROUP = ATTN_BLOCK * max(DILATIONS)
VMEM_LIMIT = 56 * 1024 * 1024

F32 = jnp.float32
BF16 = jnp.bfloat16


def _dot(a, b):
    return jnp.dot(a, b, preferred_element_type=F32)


def _dot_nt(a, b):
    return lax.dot_general(a, b, (((1,), (1,)), ((), ())), preferred_element_type=F32)


def _sigmoid(x):
    return 1.0 / (1.0 + jnp.exp(-x))


def _rms(x):
    return x * lax.rsqrt(jnp.mean(x * x, axis=-1, keepdims=True) + NORM_EPS)


def _split3(a):
    hi = a.astype(BF16)
    rest = a - hi.astype(F32)
    mid = rest.astype(BF16)
    return hi, mid, (rest - mid.astype(F32)).astype(BF16)


def _ssd_prepare(bm, cm, dt, a_row, a_col, tril):
    cs = sum(_dot(tril, piece) for piece in _split3(dt * a_row))
    dt_t = dt.T[0:2 * SSD_HEADS]
    cs_t = sum(_dot_nt(piece, tril) for piece in _split3(dt_t * a_col))
    w_t = dt_t * jnp.exp2(cs_t[:, CHUNK - 1:CHUNK] - cs_t)
    return dict(cs=cs, cs_dt_t=cs_t - jnp.log2(dt_t), w_t=w_t, cb=_dot_nt(cm, bm),
                cm_f=cm.astype(F32), bm_t=bm.astype(F32).T)


def _ssd_chunk(xs_b, pre, dskip, prev, causal, low):
    zero = jnp.zeros((CHUNK, LANES), BF16)
    ys, nxt = [], []
    for g in range(SSD_WIDTH // LANES):
        x2 = xs_b[:, g * LANES:(g + 1) * LANES]
        rhs = jnp.concatenate([x2, prev[g].astype(BF16)], axis=0)
        y_lhs, inj_lhs, dec2 = [], [], []
        for h in (2 * g, 2 * g + 1):
            cs_h = jnp.broadcast_to(pre["cs"][:, h:h + 1], (CHUNK, CHUNK))
            grow = jnp.exp2(cs_h)
            seg = jnp.exp2(jnp.where(causal, cs_h - pre["cs_dt_t"][h:h + 1, :], -jnp.inf))
            y_lhs.append(jnp.concatenate([pre["cb"] * seg, pre["cm_f"] * grow],
                                         axis=1).astype(BF16))
            inj_lhs.append((pre["bm_t"] * pre["w_t"][h:h + 1, :]).astype(BF16))
            dec2.append(grow[CHUNK - 1:CHUNK, :])
        y_pair = _dot(jnp.concatenate(y_lhs, axis=0), rhs)
        ys.append(jnp.where(low, y_pair[:CHUNK], y_pair[CHUNK:]))
        x_split = jnp.concatenate([jnp.where(low, x2, zero), jnp.where(low, zero, x2)], axis=0)
        inject = _dot(jnp.concatenate(inj_lhs, axis=1), x_split)
        nxt.append(prev[g] * jnp.where(low[0:1], dec2[0], dec2[1]) + inject)

    return (jnp.concatenate(ys, axis=1) + dskip * xs_b.astype(F32)).astype(BF16), nxt


def _in_proj_kernel(x_ref, pos_ref, invf_ref, g_ref, wq_ref, wk_ref, wv_ref, wz_ref,
                    wxbc_ref, wdt_ref, convw_ref, convb_ref, dtb_ref, alog_ref, alogc_ref,
                    dskip_ref, *rest):
    n_qkv = 3 * len(DILATIONS)
    qkv_refs = rest[:n_qkv]
    z_ref, y_ref, cbuf, qs, ks, vs = rest[n_qkv:n_qkv + 6]
    stages, (state, u_s, cos_s, sin_s, raw, act_s, dt_s) = rest[n_qkv + 6:-7], rest[-7:]
    t = x_ref.shape[1]
    j = pl.program_id(1)
    chunks = [slice(i, i + ROW_CHUNK) for i in range(0, t, ROW_CHUNK)]
    for rows in chunks:
        u_s[rows, :] = (_rms(x_ref[0, rows, :]) * g_ref[...]).astype(BF16)
    u = u_s[...]

    ang = invf_ref[...] * pos_ref[0].astype(F32)
    lane = lax.broadcasted_iota(jnp.int32, (t, LANES), 1)
    sin4 = jnp.concatenate([jnp.sin(ang)] * 4, axis=0).T
    cos_s[...] = jnp.concatenate([jnp.cos(ang)] * 4, axis=0).T
    sin_s[...] = jnp.where((lane % HEAD_DIM) < (HEAD_DIM // 2), -sin4, sin4)
    lane_c = lax.broadcasted_iota(jnp.int32, (ROW_CHUNK, LANES), 1)
    first_half = (lane_c % HEAD_DIM) < (HEAD_DIM // 2)

    def rope_rows(plane, scale):
        for rows in chunks:
            cos_c, sin_c = cos_s[rows, :], sin_s[rows, :]
            for g in range(ATTN_WIDTH // LANES):
                val = raw[rows, g * LANES:(g + 1) * LANES]
                partner = jnp.where(first_half, pltpu.roll(val, LANES - HEAD_DIM // 2, 1),
                                    pltpu.roll(val, HEAD_DIM // 2, 1))
                plane[g, rows, :] = (val * cos_c + partner * sin_c) * scale

    def emit_layouts(ti):
        planes, d_prev = (qs, ks, vs), 1
        for bi, d in enumerate(DILATIONS):
            f = d // d_prev
            staged = stages[3 * (bi - 1):3 * bi] if 0 < bi < len(DILATIONS) - 1 else (None,) * 3
            src, stage, dst = planes[ti], staged[ti], qkv_refs[3 * bi + ti]
            for r in range(d):
                start = (r % d_prev) * (t // d_prev) + r // d_prev
                for g in range(ATTN_WIDTH // LANES):
                    val = src[g] if d == 1 else src[g, pl.ds(start, t // d, stride=f), :]
                    if stage is not None:
                        stage[g, r * (t // d):(r + 1) * (t // d), :] = val
                    c = r * ATTN_WIDTH + g * LANES
                    dst[0, :, c:c + LANES] = val.astype(BF16)
            if staged[0] is not None:
                planes, d_prev = staged, d

    @pl.when(j == 0)
    def _():
        cbuf[0:CONV_HALO, :] = jnp.zeros((CONV_HALO, CONV_CH), F32)
        state[...] = jnp.zeros(state.shape, F32)

    cbuf[CONV_HALO:CONV_HALO + t, :] = _dot(u, wxbc_ref[...])
    raw[:, 0:LANES] = _dot(u, wdt_ref[...])
    w = convw_ref[...]
    for rows in chunks:
        y = convb_ref[...]
        for s in range(CONV_WIDTH):
            y = y + (w[CONV_WIDTH - 1 - s:CONV_WIDTH - s]
                     * cbuf[CONV_HALO - s + rows.start:CONV_HALO - s + rows.stop, :])
        act_s[rows, :] = (y * _sigmoid(y)).astype(BF16)
        dtr = raw[rows, 0:LANES] + dtb_ref[...]
        dt_s[rows, :] = jnp.maximum(dtr, 0.0) + jnp.log1p(jnp.exp(-jnp.abs(dtr)))
    cbuf[0:CONV_HALO, :] = cbuf[t:t + CONV_HALO, :]

    row = lax.broadcasted_iota(jnp.int32, (CHUNK, CHUNK), 0)
    col = lax.broadcasted_iota(jnp.int32, (CHUNK, CHUNK), 1)
    causal = row >= col
    low = col < SSD_HEAD_DIM
    tril = causal.astype(BF16)
    a_row = -jnp.exp(alog_ref[...]) * math.log2(math.e)
    a_col = -jnp.exp(alogc_ref[...]) * math.log2(math.e)
    groups = SSD_WIDTH // LANES
    st = [state[:, g * LANES:(g + 1) * LANES] for g in range(groups)]

    def ssd_step(c, st):
        rows = slice(c * CHUNK, (c + 1) * CHUNK)
        pre_c = _ssd_prepare(act_s[rows, SSD_WIDTH:SSD_WIDTH + SSD_STATE],
                             act_s[rows, SSD_WIDTH + SSD_STATE:], dt_s[rows, :], a_row, a_col, tril)
        yc, st = _ssd_chunk(act_s[rows, 0:SSD_WIDTH], pre_c, dskip_ref[...], st, causal, low)
        y_ref[0, rows, :] = yc
        return st

    assert t // CHUNK == 4
    raw[...] = _dot(u, wq_ref[...])
    st = ssd_step(0, st)
    rope_rows(qs, Q_SCALE)
    raw[...] = _dot(u, wk_ref[...])
    emit_layouts(0)
    st = ssd_step(1, st)
    rope_rows(ks, 1.0)
    v = _dot(u, wv_ref[...])
    emit_layouts(1)
    st = ssd_step(2, st)
    for g in range(ATTN_WIDTH // LANES):
        vs[g] = v[:, g * LANES:(g + 1) * LANES]
    z_ref[0] = _dot(u, wz_ref[...]).astype(BF16)
    emit_layouts(2)
    st = ssd_step(3, st)
    for g in range(groups):
        state[:, g * LANES:(g + 1) * LANES] = st[g]


def _in_proj(x, pos3, invf, g, wq, wk, wv, wz, wxbc, wdt, convw, convb, dtb, alog, alogc, dskip):
    b, s, _ = x.shape
    t = PROJ_TILE
    tile = lambda w_: pl.BlockSpec((1, t, w_), lambda bi, j: (bi, j, 0))
    full = lambda a: pl.BlockSpec(a.shape, lambda bi, j: (0,) * a.ndim)
    wide = jax.ShapeDtypeStruct((b, s, ATTN_WIDTH), BF16)
    qkv_specs, qkv_shapes = [], []
    for d in DILATIONS:
        qkv_specs += [pl.BlockSpec((1, t // d, d * ATTN_WIDTH), lambda bi, j: (bi, j, 0))] * 3
        qkv_shapes += [jax.ShapeDtypeStruct((b, s // d, d * ATTN_WIDTH), BF16)] * 3
    plane = pltpu.VMEM((ATTN_WIDTH // LANES, t, LANES), F32)
    return pl.pallas_call(
        _in_proj_kernel,
        grid=(b, s // t),
        in_specs=[tile(D_MODEL), pl.BlockSpec((1, 1, t), lambda bi, j: (bi, 0, j)),
                  full(invf), full(g), full(wq), full(wk), full(wv), full(wz), full(wxbc),
                  full(wdt), full(convw), full(convb), full(dtb), full(alog), full(alogc),
                  full(dskip)],
        out_specs=qkv_specs + [tile(ATTN_WIDTH), tile(SSD_WIDTH)],
        out_shape=qkv_shapes + [wide, wide],
        scratch_shapes=[pltpu.VMEM((t + CONV_HALO, CONV_CH), F32)]
                       + [plane] * (3 * (len(DILATIONS) - 1))
                       + [pltpu.VMEM((SSD_STATE, SSD_WIDTH), F32), pltpu.VMEM((t, D_MODEL), BF16),
                          pltpu.VMEM((t, LANES), F32), pltpu.VMEM((t, LANES), F32),
                          pltpu.VMEM((t, ATTN_WIDTH), F32), pltpu.VMEM((t, CONV_CH), BF16),
                          pltpu.VMEM((t, LANES), F32)],
        compiler_params=pltpu.CompilerParams(
            dimension_semantics=("arbitrary", "arbitrary"), vmem_limit_bytes=VMEM_LIMIT),
        name="in_proj",
    )(x, pos3, invf, g, wq, wk, wv, wz, wxbc, wdt, convw, convb, dtb, alog, alogc, dskip)


def _attn_block(has_prev, span, q, kp, kc, vp, vc):
    blk = ATTN_BLOCK
    qi = lax.broadcasted_iota(jnp.int32, (blk, 2 * blk), 0)
    ki = lax.broadcasted_iota(jnp.int32, (blk, 2 * blk), 1)
    dist = qi + blk - ki
    valid = (dist >= 0) & (dist <= span) & ((ki >= blk) | has_prev)
    bias = jnp.where(valid, 0.0, -jnp.inf).astype(F32)
    bias = jnp.concatenate([bias, bias], axis=0)
    low = lax.broadcasted_iota(jnp.int32, (blk, LANES), 1) < HEAD_DIM

    k = jnp.concatenate([kp, kc], axis=0)
    v = jnp.concatenate([vp, vc], axis=0)
    outs, maxes, sums = [], [], []
    for g in range(ATTN_WIDTH // LANES):
        sl = slice(g * LANES, (g + 1) * LANES)
        q2, k2, v2 = q[:, sl], k[:, sl], v[:, sl]
        zero = jnp.zeros_like(q2)
        qq = jnp.concatenate([jnp.where(low, q2, zero), jnp.where(low, zero, q2)], axis=0)
        s = _dot_nt(qq, k2) + bias
        m = jnp.max(s, axis=-1, keepdims=True)
        p = jnp.exp2(s - m)
        l = jnp.sum(p, axis=-1, keepdims=True)
        o = _dot(p.astype(BF16), v2)
        outs.append(jnp.where(low, o[:blk], o[blk:]))
        maxes.append(jnp.where(low, m[:blk], m[blk:]))
        sums.append(jnp.where(low, l[:blk], l[blk:]))
    return outs, maxes, sums


def _attn_kernel(*refs):
    nbr = len(BRANCHES)
    in_refs = refs[:5 * nbr]
    out_ref = refs[5 * nbr]
    acc_o = refs[5 * nbr + 1:5 * nbr + 1 + nbr]
    acc_m = refs[5 * nbr + 1 + nbr:5 * nbr + 1 + 2 * nbr]
    acc_l = refs[5 * nbr + 1 + 2 * nbr:]
    group_steps = ATTN_GROUP // ATTN_BLOCK
    blk, wid = ATTN_BLOCK, ATTN_WIDTH

    for sub in range(ATTN_UNROLL):
        i = pl.program_id(1) * ATTN_UNROLL + sub
        local = i % group_steps
        for bi, (window, d) in enumerate(BRANCHES):
            q_ref, kp_ref, kc_ref, vp_ref, vc_ref = in_refs[5 * bi:5 * bi + 5]
            if d == 1:
                rows = slice(sub * blk, (sub + 1) * blk)
                before = slice((sub - 1) * blk, sub * blk)
                ops = (q_ref[0, rows, :],
                       kp_ref[0] if sub == 0 else kc_ref[0, before, :], kc_ref[0, rows, :],
                       vp_ref[0] if sub == 0 else vc_ref[0, before, :], vc_ref[0, rows, :])
                has_prev = (i > 0) if sub == 0 else True
            else:
                cols = slice(sub * wid, (sub + 1) * wid)
                ops = tuple(ref[0, :, cols] for ref in (q_ref, kp_ref, kc_ref, vp_ref, vc_ref))
                has_prev = (i // d) > 0
            o, m, l = _attn_block(has_prev, window // d, *ops)
            start = (local // d) * (d * blk) + i % d
            rows = pl.ds(start, blk) if d == 1 else pl.ds(start, blk, stride=d)
            for g in range(wid // LANES):
                acc_o[bi][g, rows, :] = o[g]
                acc_m[bi][g, rows, :] = m[g]
                acc_l[bi][g, rows, :] = l[g]

    @pl.when(local == group_steps - 1)
    def _():
        total = lambda xs: functools.reduce(lambda a, b_: a + b_, xs)

        def merge(c, carry):
            rows = pl.ds(pl.multiple_of(c * MERGE_ROWS, MERGE_ROWS), MERGE_ROWS)
            for g in range(ATTN_WIDTH // LANES):
                ms = [a[g, rows, :] for a in acc_m]
                top = functools.reduce(jnp.maximum, ms)
                es = [jnp.exp2(m - top) for m in ms]
                den = total([e * a[g, rows, :] for e, a in zip(es, acc_l)])
                num = total([e * a[g, rows, :] for e, a in zip(es, acc_o)])
                out_ref[0, rows, g * LANES:(g + 1) * LANES] = (num / den).astype(BF16)
            return carry

        lax.fori_loop(0, ATTN_GROUP // MERGE_ROWS, merge, 0)


def _attention(qkv):
    b = qkv[0].shape[0]
    s = qkv[0].shape[1] * DILATIONS[0]
    un = ATTN_UNROLL
    steps = s // (ATTN_BLOCK * un)
    group_steps = ATTN_GROUP // (ATTN_BLOCK * un)
    in_specs = []
    for d in DILATIONS:
        if d == 1:
            cur = pl.BlockSpec((1, un * ATTN_BLOCK, ATTN_WIDTH), lambda bi, j: (bi, j, 0))
            prev = pl.BlockSpec((1, ATTN_BLOCK, ATTN_WIDTH),
                                lambda bi, j: (bi, jnp.maximum(j * un - 1, 0), 0))
        else:
            assert d % un == 0
            cur = pl.BlockSpec((1, ATTN_BLOCK, un * ATTN_WIDTH),
                               lambda bi, j, d=d: (bi, (j * un) // d, (j * un % d) // un))
            prev = pl.BlockSpec((1, ATTN_BLOCK, un * ATTN_WIDTH),
                                lambda bi, j, d=d: (bi, jnp.maximum((j * un) // d - 1, 0),
                                                    (j * un % d) // un))
        in_specs += [cur, prev, cur, prev, cur]
    args = []
    for bi in range(len(DILATIONS)):
        q, k, v = qkv[3 * bi:3 * bi + 3]
        args += [q, k, k, v, v]
    nbr = len(BRANCHES)
    return pl.pallas_call(
        _attn_kernel,
        grid=(b, steps),
        in_specs=in_specs,
        out_specs=pl.BlockSpec((1, ATTN_GROUP, ATTN_WIDTH), lambda bi, i: (bi, i // group_steps, 0)),
        out_shape=jax.ShapeDtypeStruct((b, s, ATTN_WIDTH), BF16),
        scratch_shapes=[pltpu.VMEM((ATTN_WIDTH // LANES, ATTN_GROUP, LANES), F32)] * (3 * nbr),
        compiler_params=pltpu.CompilerParams(
            dimension_semantics=("arbitrary", "arbitrary"), vmem_limit_bytes=VMEM_LIMIT),
        name="dilated_attention",
    )(*args)


def _tail_kernel(x_ref, attn_ref, y_ref, z_ref, p_ref, gssd_ref, gmix_ref, wout_ref, gpre_ref,
                 wup_ref, wdown_ref, gmlp_ref, wgate_ref, wproj_ref, gple_ref, out_ref):
    def gate_ssd(d):
        zf = z_ref[d["rows"], :].astype(F32)
        y = y_ref[d["rows"], :].astype(F32) * (zf * _sigmoid(zf))
        d["y"] = (_rms(y) * gssd_ref[...]).astype(BF16)

    def out_proj(d):
        d["mix"] = (_dot(attn_ref[d["rows"], :], wout_ref[0:ATTN_WIDTH, :])
                    + _dot(d["y"], wout_ref[ATTN_WIDTH:, :]))

    def norm_mix(d):
        d["h"] = x_ref[d["rows"], :] + _rms(d["mix"]) * gmix_ref[...]
        d["u"] = (_rms(d["h"]) * gpre_ref[...]).astype(BF16)

    def mlp(d):
        ff = jnp.zeros(d["h"].shape, F32)
        for c in range(D_FF // FF_CHUNK):
            sl = slice(c * FF_CHUNK, (c + 1) * FF_CHUNK)
            hid = jnp.square(jnp.maximum(_dot(d["u"], wup_ref[:, sl]), 0.0)).astype(BF16)
            ff = ff + _dot(hid, wdown_ref[sl, :])
        d["ff"] = ff

    def norm_mlp(d):
        d["h"] = d["h"] + _rms(d["ff"]) * gmlp_ref[...]

    def ple_proj(d):
        d["gate"] = _dot(d["h"].astype(BF16), wgate_ref[...])
        d["ple"] = _dot(p_ref[d["rows"], :].astype(BF16), wproj_ref[...])

    def norm_ple(d):
        ple = d["ple"] * _sigmoid(d["gate"])
        out_ref[d["rows"], :] = d["h"] + _rms(ple) * gple_ref[...]

    half = x_ref.shape[0] // 2
    a = dict(rows=slice(0, half))
    b = dict(rows=slice(half, 2 * half))
    for stage, d in ((gate_ssd, a), (out_proj, a), (gate_ssd, b), (norm_mix, a), (out_proj, b),
                     (mlp, a), (norm_mix, b), (mlp, b), (norm_mlp, a), (ple_proj, a),
                     (norm_mlp, b), (norm_ple, a), (ple_proj, b), (norm_ple, b)):
        stage(d)


def _tail(x2, attn2, y2, z2, p2, gssd, gmix, wout, gpre, wup, wdown, gmlp, wgate, wproj, gple):
    n = x2.shape[0]
    t = TAIL_TILE
    tile = lambda w_: pl.BlockSpec((t, w_), lambda i: (i, 0))
    full = lambda a: pl.BlockSpec(a.shape, lambda i: (0, 0), pipeline_mode=pl.Buffered(1))
    return pl.pallas_call(
        _tail_kernel,
        grid=(n // t,),
        in_specs=[tile(D_MODEL), tile(ATTN_WIDTH), tile(SSD_WIDTH), tile(SSD_WIDTH),
                  tile(PLE_DIM), full(gssd), full(gmix),
                  full(wout), full(gpre), full(wup), full(wdown), full(gmlp), full(wgate),
                  full(wproj), full(gple)],
        out_specs=tile(D_MODEL),
        out_shape=jax.ShapeDtypeStruct((n, D_MODEL), F32),
        compiler_params=pltpu.CompilerParams(
            dimension_semantics=("arbitrary",), vmem_limit_bytes=VMEM_LIMIT),
        name="tail",
    )(x2, attn2, y2, z2, p2, gssd, gmix, wout, gpre, wup, wdown, gmlp, wgate, wproj, gple)


def _pad_lanes(a):
    return jnp.pad(a, ((0, 0), (0, LANES - a.shape[-1])))


def kernel(x, p, positions, norm_mix_pre, norm_mix_post, w_in, conv_w, conv_b, dt_bias, a_log,
           d_skip, ssd_norm_g, w_out, norm_mlp_pre, norm_mlp_post, w_up, w_down, w_ple_gate,
           w_ple_proj, norm_ple_post):
    b, s, _ = x.shape
    half = HEAD_DIM // 2
    invf = (ROPE_THETA ** (-jnp.arange(half, dtype=F32) * 2.0 / HEAD_DIM)).reshape(half, 1)
    pos3 = positions.reshape(b, 1, s)
    row = lambda a: a.reshape(1, -1)

    h = x
    for i in range(w_in.shape[0]):
        wi = w_in[i].astype(BF16)
        c0 = 3 * ATTN_WIDTH
        c1 = c0 + SSD_WIDTH
        c2 = c1 + CONV_CH
        *qkv, z, y = _in_proj(
            h, pos3, invf, row(norm_mix_pre[i]),
            wi[:, 0:ATTN_WIDTH], wi[:, ATTN_WIDTH:2 * ATTN_WIDTH], wi[:, 2 * ATTN_WIDTH:c0],
            wi[:, c0:c1], wi[:, c1:c2], _pad_lanes(wi[:, c2:]),
            conv_w[i], row(conv_b[i]), _pad_lanes(row(dt_bias[i])), _pad_lanes(row(a_log[i])),
            jnp.pad(a_log[i], (0, SSD_HEADS)).reshape(2 * SSD_HEADS, 1),
            row(jnp.repeat(d_skip[i], SSD_HEAD_DIM)))
        attn = _attention(qkv)
        h = _tail(h.reshape(b * s, D_MODEL), attn.reshape(b * s, ATTN_WIDTH),
                  y.reshape(b * s, SSD_WIDTH), z.reshape(b * s, SSD_WIDTH),
                  p[i].reshape(b * s, PLE_DIM), row(ssd_norm_g[i]),
                  row(norm_mix_post[i]), w_out[i].astype(BF16), row(norm_mlp_pre[i]),
                  w_up[i].astype(BF16), w_down[i].astype(BF16), row(norm_mlp_post[i]),
                  w_ple_gate[i].astype(BF16), w_ple_proj[i].astype(BF16),
                  row(norm_ple_post[i])).reshape(b, s, D_MODEL)
    return h
```

```python
import functools
import math

import jax
import jax.numpy as jnp
from jax import lax
from jax.experimental import pallas as pl
from jax.experimental.pallas import tpu as pltpu

D_MODEL = 1024
PLE_DIM = 256
N_HEADS = 8
HEAD_DIM = 64
ATTN_WIDTH = N_HEADS * HEAD_DIM
SSD_HEADS = 8
SSD_HEAD_DIM = 64
SSD_WIDTH = SSD_HEADS * SSD_HEAD_DIM
SSD_STATE = 128
CONV_WIDTH = 4
CONV_CH = SSD_WIDTH + 2 * SSD_STATE
CHUNK = 128
D_FF = 4 * D_MODEL
ROPE_THETA = 10000.0
BRANCHES = ((128, 1), (512, 4), (2048, 16))
DILATIONS = tuple(d for _, d in BRANCHES)
ATTN_BLOCK = 128
NORM_EPS = 1e-6
Q_SCALE = HEAD_DIM ** -0.5 * math.log2(math.e)

LANES = 128
CONV_HALO = 8
PROJ_TILE = 512
TAIL_TILE = 512
FF_CHUNK = 1024
MERGE_ROWS = 256
ATTN_UNROLL = 4
ATTN_GROUP = ATTN_BLOCK * max(DILATIONS)
VMEM_LIMIT = 56 * 1024 * 1024

F32 = jnp.float32
BF16 = jnp.bfloat16


def _dot(a, b):
    return jnp.dot(a, b, preferred_element_type=F32)


def _dot_nt(a, b):
    return lax.dot_general(a, b, (((1,), (1,)), ((), ())), preferred_element_type=F32)


def _sigmoid(x):
    return 1.0 / (1.0 + jnp.exp(-x))


def _rms(x):
    return x * lax.rsqrt(jnp.mean(x * x, axis=-1, keepdims=True) + NORM_EPS)


def _split3(a):
    hi = a.astype(BF16)
    rest = a - hi.astype(F32)
    mid = rest.astype(BF16)
    return hi, mid, (rest - mid.astype(F32)).astype(BF16)


def _ssd_prepare(bm, cm, dt, a_row, a_col, tril):
    cs = sum(_dot(tril, piece) for piece in _split3(dt * a_row))
    dt_t = dt.T[0:2 * SSD_HEADS]
    cs_t = sum(_dot_nt(piece, tril) for piece in _split3(dt_t * a_col))
    w_t = dt_t * jnp.exp2(cs_t[:, CHUNK - 1:CHUNK] - cs_t)
    return dict(cs=cs, cs_dt_t=cs_t - jnp.log2(dt_t), w_t=w_t, cb=_dot_nt(cm, bm),
                cm_f=cm.astype(F32), bm_t=bm.astype(F32).T)


def _ssd_chunk(xs_b, pre, dskip, prev, causal, low):
    zero = jnp.zeros((CHUNK, LANES), BF16)
    ys, nxt = [], []
    for g in range(SSD_WIDTH // LANES):
        x2 = xs_b[:, g * LANES:(g + 1) * LANES]
        rhs = jnp.concatenate([x2, prev[g].astype(BF16)], axis=0)
        y_lhs, inj_lhs, dec2 = [], [], []
        for h in (2 * g, 2 * g + 1):
            cs_h = jnp.broadcast_to(pre["cs"][:, h:h + 1], (CHUNK, CHUNK))
            grow = jnp.exp2(cs_h)
            seg = jnp.exp2(jnp.where(causal, cs_h - pre["cs_dt_t"][h:h + 1, :], -jnp.inf))
            y_lhs.append(jnp.concatenate([pre["cb"] * seg, pre["cm_f"] * grow],
                                         axis=1).astype(BF16))
            inj_lhs.append((pre["bm_t"] * pre["w_t"][h:h + 1, :]).astype(BF16))
            dec2.append(grow[CHUNK - 1:CHUNK, :])
        y_pair = _dot(jnp.concatenate(y_lhs, axis=0), rhs)
        ys.append(jnp.where(low, y_pair[:CHUNK], y_pair[CHUNK:]))
        x_split = jnp.concatenate([jnp.where(low, x2, zero), jnp.where(low, zero, x2)], axis=0)
        inject = _dot(jnp.concatenate(inj_lhs, axis=1), x_split)
        nxt.append(prev[g] * jnp.where(low[0:1], dec2[0], dec2[1]) + inject)

    return (jnp.concatenate(ys, axis=1) + dskip * xs_b.astype(F32)).astype(BF16), nxt


def _in_proj_kernel(x_ref, pos_ref, invf_ref, g_ref, win_ref, convw_ref, convb_ref, dtb_ref,
                    alog_ref, alogc_ref, dskip_ref, *rest):
    widths = (ATTN_WIDTH, ATTN_WIDTH, ATTN_WIDTH, SSD_WIDTH, CONV_CH, SSD_HEADS)
    edges = [sum(widths[:c]) for c in range(len(widths) + 1)]
    wq_ref, wk_ref, wv_ref, wz_ref, wxbc_ref, wdt_ref = (
        win_ref.at[:, lo:hi] for lo, hi in zip(edges, edges[1:]))
    n_qkv = 3 * len(DILATIONS)
    qkv_refs = rest[:n_qkv]
    z_ref, y_ref, cbuf, qs, ks, vs = rest[n_qkv:n_qkv + 6]
    stages, state = rest[n_qkv + 6:-1], rest[-1]
    t = x_ref.shape[1]
    j = pl.program_id(1)
    u = (_rms(x_ref[0]) * g_ref[...]).astype(BF16)

    ang = invf_ref[...] * pos_ref[0].astype(F32)
    cos4 = jnp.concatenate([jnp.cos(ang)] * 4, axis=0).T
    sin4 = jnp.concatenate([jnp.sin(ang)] * 4, axis=0).T
    lane = lax.broadcasted_iota(jnp.int32, (t, LANES), 1)
    first_half = (lane % HEAD_DIM) < (HEAD_DIM // 2)
    sin_signed = jnp.where(first_half, -sin4, sin4)

    def rope(v):
        partner = jnp.where(first_half, pltpu.roll(v, LANES - HEAD_DIM // 2, 1),
                            pltpu.roll(v, HEAD_DIM // 2, 1))
        return v * cos4 + partner * sin_signed

    def emit_layouts(ti):
        planes, d_prev = (qs, ks, vs), 1
        for bi, d in enumerate(DILATIONS):
            f = d // d_prev
            staged = stages[3 * (bi - 1):3 * bi] if 0 < bi < len(DILATIONS) - 1 else (None,) * 3
            src, stage, dst = planes[ti], staged[ti], qkv_refs[3 * bi + ti]
            for r in range(d):
                start = (r % d_prev) * (t // d_prev) + r // d_prev
                for g in range(ATTN_WIDTH // LANES):
                    val = src[g] if d == 1 else src[g, pl.ds(start, t // d, stride=f), :]
                    if stage is not None:
                        stage[g, r * (t // d):(r + 1) * (t // d), :] = val
                    c = r * ATTN_WIDTH + g * LANES
                    dst[0, :, c:c + LANES] = val.astype(BF16)
            if staged[0] is not None:
                planes, d_prev = staged, d

    xbc = _dot(u, wxbc_ref[...])
    dtr = jnp.pad(_dot(u, wdt_ref[...]), ((0, 0), (0, LANES - SSD_HEADS))) + dtb_ref[...]

    @pl.when(j == 0)
    def _():
        cbuf[0:CONV_HALO, :] = jnp.zeros((CONV_HALO, CONV_CH), F32)
        state[...] = jnp.zeros(state.shape, F32)

    cbuf[CONV_HALO:CONV_HALO + t, :] = xbc
    w = convw_ref[...]
    y = convb_ref[...] + w[CONV_WIDTH - 1:CONV_WIDTH] * xbc
    for s in range(1, CONV_WIDTH):
        y = y + w[CONV_WIDTH - 1 - s:CONV_WIDTH - s] * cbuf[CONV_HALO - s:CONV_HALO - s + t, :]
    cbuf[0:CONV_HALO, :] = cbuf[t:t + CONV_HALO, :]
    act = y * _sigmoid(y)
    xs_b = act[:, :SSD_WIDTH].astype(BF16)
    bm = act[:, SSD_WIDTH:SSD_WIDTH + SSD_STATE].astype(BF16)
    cm = act[:, SSD_WIDTH + SSD_STATE:].astype(BF16)

    dt = jnp.maximum(dtr, 0.0) + jnp.log1p(jnp.exp(-jnp.abs(dtr)))

    row = lax.broadcasted_iota(jnp.int32, (CHUNK, CHUNK), 0)
    col = lax.broadcasted_iota(jnp.int32, (CHUNK, CHUNK), 1)
    causal = row >= col
    low = col < SSD_HEAD_DIM
    tril = causal.astype(BF16)
    a_row = -jnp.exp(alog_ref[...]) * math.log2(math.e)
    a_col = -jnp.exp(alogc_ref[...]) * math.log2(math.e)
    groups = SSD_WIDTH // LANES
    st = [state[:, g * LANES:(g + 1) * LANES] for g in range(groups)]

    def ssd_step(c, st):
        rows = slice(c * CHUNK, (c + 1) * CHUNK)
        pre_c = _ssd_prepare(bm[rows], cm[rows], dt[rows], a_row, a_col, tril)
        yc, st = _ssd_chunk(xs_b[rows], pre_c, dskip_ref[...], st, causal, low)
        y_ref[0, rows, :] = yc
        return st

    lanes = [slice(g * LANES, (g + 1) * LANES) for g in range(ATTN_WIDTH // LANES)]
    assert t // CHUNK == 4
    q = _dot(u, wq_ref[...])
    st = ssd_step(0, st)
    for g, sl in enumerate(lanes):
        qs[g] = rope(q[:, sl]) * Q_SCALE
    k = _dot(u, wk_ref[...])
    emit_layouts(0)
    st = ssd_step(1, st)
    for g, sl in enumerate(lanes):
        ks[g] = rope(k[:, sl])
    v = _dot(u, wv_ref[...])
    emit_layouts(1)
    st = ssd_step(2, st)
    for g, sl in enumerate(lanes):
        vs[g] = v[:, sl]
    z_ref[0] = _dot(u, wz_ref[...]).astype(BF16)
    emit_layouts(2)
    st = ssd_step(3, st)
    for g in range(groups):
        state[:, g * LANES:(g + 1) * LANES] = st[g]


def _in_proj(x, pos3, invf, g, win, convw, convb, dtb, alog, alogc, dskip):
    b, s, _ = x.shape
    t = PROJ_TILE
    tile = lambda w_: pl.BlockSpec((1, t, w_), lambda bi, j: (bi, j, 0))
    full = lambda a: pl.BlockSpec(a.shape, lambda bi, j: (0,) * a.ndim)
    wide = jax.ShapeDtypeStruct((b, s, ATTN_WIDTH), BF16)
    qkv_specs, qkv_shapes = [], []
    for d in DILATIONS:
        qkv_specs += [pl.BlockSpec((1, t // d, d * ATTN_WIDTH), lambda bi, j: (bi, j, 0))] * 3
        qkv_shapes += [jax.ShapeDtypeStruct((b, s // d, d * ATTN_WIDTH), BF16)] * 3
    plane = pltpu.VMEM((ATTN_WIDTH // LANES, t, LANES), F32)
    return pl.pallas_call(
        _in_proj_kernel,
        grid=(b, s // t),
        in_specs=[tile(D_MODEL), pl.BlockSpec((1, 1, t), lambda bi, j: (bi, 0, j)),
                  full(invf), full(g), full(win), full(convw), full(convb), full(dtb), full(alog), full(alogc),
                  full(dskip)],
        out_specs=qkv_specs + [tile(ATTN_WIDTH), tile(SSD_WIDTH)],
        out_shape=qkv_shapes + [wide, wide],
        scratch_shapes=[pltpu.VMEM((t + CONV_HALO, CONV_CH), F32)]
                       + [plane] * (3 * (len(DILATIONS) - 1))
                       + [pltpu.VMEM((SSD_STATE, SSD_WIDTH), F32)],
        compiler_params=pltpu.CompilerParams(
            dimension_semantics=("arbitrary", "arbitrary"), vmem_limit_bytes=VMEM_LIMIT),
        name="in_proj",
    )(x, pos3, invf, g, win, convw, convb, dtb, alog, alogc, dskip)


def _attn_block(has_prev, span, q, kp, kc, vp, vc):
    blk = ATTN_BLOCK
    qi = lax.broadcasted_iota(jnp.int32, (blk, 2 * blk), 0)
    ki = lax.broadcasted_iota(jnp.int32, (blk, 2 * blk), 1)
    dist = qi + blk - ki
    valid = (dist >= 0) & (dist <= span) & ((ki >= blk) | has_prev)
    bias = jnp.where(valid, 0.0, -jnp.inf).astype(F32)
    bias = jnp.concatenate([bias, bias], axis=0)
    low = lax.broadcasted_iota(jnp.int32, (blk, LANES), 1) < HEAD_DIM

    k = jnp.concatenate([kp, kc], axis=0)
    v = jnp.concatenate([vp, vc], axis=0)
    outs, maxes, sums = [], [], []
    for g in range(ATTN_WIDTH // LANES):
        sl = slice(g * LANES, (g + 1) * LANES)
        q2, k2, v2 = q[:, sl], k[:, sl], v[:, sl]
        zero = jnp.zeros_like(q2)
        qq = jnp.concatenate([jnp.where(low, q2, zero), jnp.where(low, zero, q2)], axis=0)
        s = _dot_nt(qq, k2) + bias
        m = jnp.max(s, axis=-1, keepdims=True)
        p = jnp.exp2(s - m)
        l = jnp.sum(p, axis=-1, keepdims=True)
        o = _dot(p.astype(BF16), v2)
        outs.append(jnp.where(low, o[:blk], o[blk:]))
        maxes.append(jnp.where(low, m[:blk], m[blk:]))
        sums.append(jnp.where(low, l[:blk], l[blk:]))
    return outs, maxes, sums


def _attn_kernel(*refs):
    nbr = len(BRANCHES)
    in_refs = refs[:5 * nbr]
    out_ref = refs[5 * nbr]
    acc_o = refs[5 * nbr + 1:5 * nbr + 1 + nbr]
    acc_m = refs[5 * nbr + 1 + nbr:5 * nbr + 1 + 2 * nbr]
    acc_l = refs[5 * nbr + 1 + 2 * nbr:]
    group_steps = ATTN_GROUP // ATTN_BLOCK
    blk, wid = ATTN_BLOCK, ATTN_WIDTH

    for sub in range(ATTN_UNROLL):
        i = pl.program_id(1) * ATTN_UNROLL + sub
        local = i % group_steps
        for bi, (window, d) in enumerate(BRANCHES):
            q_ref, kp_ref, kc_ref, vp_ref, vc_ref = in_refs[5 * bi:5 * bi + 5]
            if d == 1:
                rows = slice(sub * blk, (sub + 1) * blk)
                before = slice((sub - 1) * blk, sub * blk)
                ops = (q_ref[0, rows, :],
                       kp_ref[0] if sub == 0 else kc_ref[0, before, :], kc_ref[0, rows, :],
                       vp_ref[0] if sub == 0 else vc_ref[0, before, :], vc_ref[0, rows, :])
                has_prev = (i > 0) if sub == 0 else True
            else:
                cols = slice(sub * wid, (sub + 1) * wid)
                ops = tuple(ref[0, :, cols] for ref in (q_ref, kp_ref, kc_ref, vp_ref, vc_ref))
                has_prev = (i // d) > 0
            o, m, l = _attn_block(has_prev, window // d, *ops)
            start = (local // d) * (d * blk) + i % d
            rows = pl.ds(start, blk) if d == 1 else pl.ds(start, blk, stride=d)
            for g in range(wid // LANES):
                acc_o[bi][g, rows, :] = o[g]
                acc_m[bi][g, rows, :] = m[g]
                acc_l[bi][g, rows, :] = l[g]

    @pl.when(local == group_steps - 1)
    def _():
        total = lambda xs: functools.reduce(lambda a, b_: a + b_, xs)

        def merge(c, carry):
            rows = pl.ds(pl.multiple_of(c * MERGE_ROWS, MERGE_ROWS), MERGE_ROWS)
            for g in range(ATTN_WIDTH // LANES):
                ms = [a[g, rows, :] for a in acc_m]
                top = functools.reduce(jnp.maximum, ms)
                es = [jnp.exp2(m - top) for m in ms]
                den = total([e * a[g, rows, :] for e, a in zip(es, acc_l)])
                num = total([e * a[g, rows, :] for e, a in zip(es, acc_o)])
                out_ref[0, rows, g * LANES:(g + 1) * LANES] = (num / den).astype(BF16)
            return carry

        lax.fori_loop(0, ATTN_GROUP // MERGE_ROWS, merge, 0)


def _attention(qkv):
    b = qkv[0].shape[0]
    s = qkv[0].shape[1] * DILATIONS[0]
    un = ATTN_UNROLL
    steps = s // (ATTN_BLOCK * un)
    group_steps = ATTN_GROUP // (ATTN_BLOCK * un)
    in_specs = []
    for d in DILATIONS:
        if d == 1:
            cur = pl.BlockSpec((1, un * ATTN_BLOCK, ATTN_WIDTH), lambda bi, j: (bi, j, 0))
            prev = pl.BlockSpec((1, ATTN_BLOCK, ATTN_WIDTH),
                                lambda bi, j: (bi, jnp.maximum(j * un - 1, 0), 0))
        else:
            assert d % un == 0
            cur = pl.BlockSpec((1, ATTN_BLOCK, un * ATTN_WIDTH),
                               lambda bi, j, d=d: (bi, (j * un) // d, (j * un % d) // un))
            prev = pl.BlockSpec((1, ATTN_BLOCK, un * ATTN_WIDTH),
                                lambda bi, j, d=d: (bi, jnp.maximum((j * un) // d - 1, 0),
                                                    (j * un % d) // un))
        in_specs += [cur, prev, cur, prev, cur]
    args = []
    for bi in range(len(DILATIONS)):
        q, k, v = qkv[3 * bi:3 * bi + 3]
        args += [q, k, k, v, v]
    nbr = len(BRANCHES)
    return pl.pallas_call(
        _attn_kernel,
        grid=(b, steps),
        in_specs=in_specs,
        out_specs=pl.BlockSpec((1, ATTN_GROUP, ATTN_WIDTH), lambda bi, i: (bi, i // group_steps, 0)),
        out_shape=jax.ShapeDtypeStruct((b, s, ATTN_WIDTH), BF16),
        scratch_shapes=[pltpu.VMEM((ATTN_WIDTH // LANES, ATTN_GROUP, LANES), F32)] * (3 * nbr),
        compiler_params=pltpu.CompilerParams(
            dimension_semantics=("arbitrary", "arbitrary"), vmem_limit_bytes=VMEM_LIMIT),
        name="dilated_attention",
    )(*args)


def _tail_kernel(x_ref, attn_ref, y_ref, z_ref, p_ref, gssd_ref, gmix_ref, wout_ref, gpre_ref,
                 wup_ref, wdown_ref, gmlp_ref, wgate_ref, wproj_ref, gple_ref, out_ref):
    def gate_ssd(d):
        zf = z_ref[d["rows"], :].astype(F32)
        y = y_ref[d["rows"], :].astype(F32) * (zf * _sigmoid(zf))
        d["y"] = (_rms(y) * gssd_ref[...]).astype(BF16)

    def out_proj(d):
        d["mix"] = (_dot(attn_ref[d["rows"], :], wout_ref[0:ATTN_WIDTH, :])
                    + _dot(d["y"], wout_ref[ATTN_WIDTH:, :]))

    def norm_mix(d):
        d["h"] = x_ref[d["rows"], :] + _rms(d["mix"]) * gmix_ref[...]
        d["u"] = (_rms(d["h"]) * gpre_ref[...]).astype(BF16)

    def mlp(d):
        ff = jnp.zeros(d["h"].shape, F32)
        for c in range(D_FF // FF_CHUNK):
            sl = slice(c * FF_CHUNK, (c + 1) * FF_CHUNK)
            hid = jnp.square(jnp.maximum(_dot(d["u"], wup_ref[:, sl]), 0.0)).astype(BF16)
            ff = ff + _dot(hid, wdown_ref[sl, :])
        d["ff"] = ff

    def norm_mlp(d):
        d["h"] = d["h"] + _rms(d["ff"]) * gmlp_ref[...]

    def ple_proj(d):
        d["gate"] = _dot(d["h"].astype(BF16), wgate_ref[...])
        d["ple"] = _dot(p_ref[d["rows"], :].astype(BF16), wproj_ref[...])

    def norm_ple(d):
        ple = d["ple"] * _sigmoid(d["gate"])
        out_ref[d["rows"], :] = d["h"] + _rms(ple) * gple_ref[...]

    half = x_ref.shape[0] // 2
    a = dict(rows=slice(0, half))
    b = dict(rows=slice(half, 2 * half))
    for stage, d in ((gate_ssd, a), (out_proj, a), (gate_ssd, b), (norm_mix, a), (out_proj, b),
                     (mlp, a), (norm_mix, b), (mlp, b), (norm_mlp, a), (ple_proj, a),
                     (norm_mlp, b), (norm_ple, a), (ple_proj, b), (norm_ple, b)):
        stage(d)


def _tail(x2, attn2, y2, z2, p2, gssd, gmix, wout, gpre, wup, wdown, gmlp, wgate, wproj, gple):
    n = x2.shape[0]
    t = TAIL_TILE
    tile = lambda w_: pl.BlockSpec((t, w_), lambda i: (i, 0))
    full = lambda a: pl.BlockSpec(a.shape, lambda i: (0, 0), pipeline_mode=pl.Buffered(1))
    return pl.pallas_call(
        _tail_kernel,
        grid=(n // t,),
        in_specs=[tile(D_MODEL), tile(ATTN_WIDTH), tile(SSD_WIDTH), tile(SSD_WIDTH),
                  tile(PLE_DIM), full(gssd), full(gmix),
                  full(wout), full(gpre), full(wup), full(wdown), full(gmlp), full(wgate),
                  full(wproj), full(gple)],
        out_specs=tile(D_MODEL),
        out_shape=jax.ShapeDtypeStruct((n, D_MODEL), F32),
        compiler_params=pltpu.CompilerParams(
            dimension_semantics=("arbitrary",), vmem_limit_bytes=VMEM_LIMIT),
        name="tail",
    )(x2, attn2, y2, z2, p2, gssd, gmix, wout, gpre, wup, wdown, gmlp, wgate, wproj, gple)


def _pad_lanes(a):
    return jnp.pad(a, ((0, 0), (0, LANES - a.shape[-1])))


def kernel(x, p, positions, norm_mix_pre, norm_mix_post, w_in, conv_w, conv_b, dt_bias, a_log,
           d_skip, ssd_norm_g, w_out, norm_mlp_pre, norm_mlp_post, w_up, w_down, w_ple_gate,
           w_ple_proj, norm_ple_post):
    b, s, _ = x.shape
    half = HEAD_DIM // 2
    invf = (ROPE_THETA ** (-jnp.arange(half, dtype=F32) * 2.0 / HEAD_DIM)).reshape(half, 1)
    pos3 = positions.reshape(b, 1, s)
    row = lambda a: a.reshape(1, -1)

    h = x
    for i in range(w_in.shape[0]):
        *qkv, z, y = _in_proj(
            h, pos3, invf, row(norm_mix_pre[i]), w_in[i].astype(BF16),
            conv_w[i], row(conv_b[i]), _pad_lanes(row(dt_bias[i])), _pad_lanes(row(a_log[i])),
            jnp.pad(a_log[i], (0, SSD_HEADS)).reshape(2 * SSD_HEADS, 1),
            row(jnp.repeat(d_skip[i], SSD_HEAD_DIM)))
        attn = _attention(qkv)
        h = _tail(h.reshape(b * s, D_MODEL), attn.reshape(b * s, ATTN_WIDTH),
                  y.reshape(b * s, SSD_WIDTH), z.reshape(b * s, SSD_WIDTH),
                  p[i].reshape(b * s, PLE_DIM), row(ssd_norm_g[i]),
                  row(norm_mix_post[i]), w_out[i].astype(BF16), row(norm_mlp_pre[i]),
                  w_up[i].astype(BF16), w_down[i].astype(BF16), row(norm_mlp_post[i]),
                  w_ple_gate[i].astype(BF16), w_ple_proj[i].astype(BF16),
                  row(norm_ple_post[i])).reshape(b, s, D_MODEL)
    return h
```

```python
import functools
import math

import jax
import jax.numpy as jnp
from jax import lax
from jax.experimental import pallas as pl
from jax.experimental.pallas import tpu as pltpu

D_MODEL = 1024
PLE_DIM = 256
N_HEADS = 8
HEAD_DIM = 64
ATTN_WIDTH = N_HEADS * HEAD_DIM
SSD_HEADS = 8
SSD_HEAD_DIM = 64
SSD_WIDTH = SSD_HEADS * SSD_HEAD_DIM
SSD_STATE = 128
CONV_WIDTH = 4
CONV_CH = SSD_WIDTH + 2 * SSD_STATE
CHUNK = 128
D_FF = 4 * D_MODEL
ROPE_THETA = 10000.0
BRANCHES = ((128, 1), (512, 4), (2048, 16))
DILATIONS = tuple(d for _, d in BRANCHES)
ATTN_BLOCK = 128
NORM_EPS = 1e-6
Q_SCALE = HEAD_DIM ** -0.5 * math.log2(math.e)

LANES = 128
CONV_HALO = 8
PROJ_TILE = 512
TAIL_TILE = 512
FF_CHUNK = 1024
CAST_STEPS = 8
MERGE_ROWS = 256
ATTN_UNROLL = 4
ATTN_GROUP = ATTN_BLOCK * max(DILATIONS)
VMEM_LIMIT = 56 * 1024 * 1024

F32 = jnp.float32
BF16 = jnp.bfloat16


def _dot(a, b):
    return jnp.dot(a, b, preferred_element_type=F32)


def _dot_nt(a, b):
    return lax.dot_general(a, b, (((1,), (1,)), ((), ())), preferred_element_type=F32)


def _sigmoid(x):
    return 1.0 / (1.0 + jnp.exp(-x))


def _rms(x):
    return x * lax.rsqrt(jnp.mean(x * x, axis=-1, keepdims=True) + NORM_EPS)


def _split3(a):
    hi = a.astype(BF16)
    rest = a - hi.astype(F32)
    mid = rest.astype(BF16)
    return hi, mid, (rest - mid.astype(F32)).astype(BF16)


def _ssd_prepare(bm, cm, dt, a_row, a_col, tril):
    cs = sum(_dot(tril, piece) for piece in _split3(dt * a_row))
    dt_t = dt.T[0:2 * SSD_HEADS]
    cs_t = sum(_dot_nt(piece, tril) for piece in _split3(dt_t * a_col))
    w_t = dt_t * jnp.exp2(cs_t[:, CHUNK - 1:CHUNK] - cs_t)
    return dict(cs=cs, cs_dt_t=cs_t - jnp.log2(dt_t), w_t=w_t, cb=_dot_nt(cm, bm),
                cm_f=cm.astype(F32), bm_t=bm.astype(F32).T)


def _ssd_chunk(xs_b, pre, dskip, prev, causal, low):
    zero = jnp.zeros((CHUNK, LANES), BF16)
    ys, nxt = [], []
    for g in range(SSD_WIDTH // LANES):
        x2 = xs_b[:, g * LANES:(g + 1) * LANES]
        rhs = jnp.concatenate([x2, prev[g].astype(BF16)], axis=0)
        y_lhs, inj_lhs, dec2 = [], [], []
        for h in (2 * g, 2 * g + 1):
            cs_h = jnp.broadcast_to(pre["cs"][:, h:h + 1], (CHUNK, CHUNK))
            grow = jnp.exp2(cs_h)
            seg = jnp.exp2(jnp.where(causal, cs_h - pre["cs_dt_t"][h:h + 1, :], -jnp.inf))
            y_lhs.append(jnp.concatenate([pre["cb"] * seg, pre["cm_f"] * grow],
                                         axis=1).astype(BF16))
            inj_lhs.append((pre["bm_t"] * pre["w_t"][h:h + 1, :]).astype(BF16))
            dec2.append(grow[CHUNK - 1:CHUNK, :])
        y_pair = _dot(jnp.concatenate(y_lhs, axis=0), rhs)
        ys.append(jnp.where(low, y_pair[:CHUNK], y_pair[CHUNK:]))
        x_split = jnp.concatenate([jnp.where(low, x2, zero), jnp.where(low, zero, x2)], axis=0)
        inject = _dot(jnp.concatenate(inj_lhs, axis=1), x_split)
        nxt.append(prev[g] * jnp.where(low[0:1], dec2[0], dec2[1]) + inject)

    return (jnp.concatenate(ys, axis=1) + dskip * xs_b.astype(F32)).astype(BF16), nxt


def _in_proj_kernel(x_ref, pos_ref, invf_ref, g_ref, win_ref, convw_ref, convb_ref, dtb_ref,
                    alog_ref, alogc_ref, dskip_ref, *rest):
    widths = (ATTN_WIDTH, ATTN_WIDTH, ATTN_WIDTH, SSD_WIDTH, CONV_CH, SSD_HEADS)
    edges = [sum(widths[:c]) for c in range(len(widths) + 1)]
    wq_ref, wk_ref, wv_ref, wz_ref, wxbc_ref, wdt_ref = (
        win_ref.at[:, lo:hi] for lo, hi in zip(edges, edges[1:]))
    n_qkv = 3 * len(DILATIONS)
    qkv_refs = rest[:n_qkv]
    z_ref, y_ref, cbuf, qs, ks, vs = rest[n_qkv:n_qkv + 6]
    stages, state = rest[n_qkv + 6:-1], rest[-1]
    t = x_ref.shape[1]
    j = pl.program_id(1)
    u = (_rms(x_ref[0]) * g_ref[...]).astype(BF16)

    ang = invf_ref[...] * pos_ref[0].astype(F32)
    cos4 = jnp.concatenate([jnp.cos(ang)] * 4, axis=0).T
    sin4 = jnp.concatenate([jnp.sin(ang)] * 4, axis=0).T
    lane = lax.broadcasted_iota(jnp.int32, (t, LANES), 1)
    first_half = (lane % HEAD_DIM) < (HEAD_DIM // 2)
    sin_signed = jnp.where(first_half, -sin4, sin4)

    def rope(v):
        partner = jnp.where(first_half, pltpu.roll(v, LANES - HEAD_DIM // 2, 1),
                            pltpu.roll(v, HEAD_DIM // 2, 1))
        return v * cos4 + partner * sin_signed

    def emit_layouts(ti):
        planes, d_prev = (qs, ks, vs), 1
        for bi, d in enumerate(DILATIONS):
            f = d // d_prev
            staged = stages[3 * (bi - 1):3 * bi] if 0 < bi < len(DILATIONS) - 1 else (None,) * 3
            src, stage, dst = planes[ti], staged[ti], qkv_refs[3 * bi + ti]
            for r in range(d):
                start = (r % d_prev) * (t // d_prev) + r // d_prev
                for g in range(ATTN_WIDTH // LANES):
                    val = src[g] if d == 1 else src[g, pl.ds(start, t // d, stride=f), :]
                    if stage is not None:
                        stage[g, r * (t // d):(r + 1) * (t // d), :] = val
                    c = r * ATTN_WIDTH + g * LANES
                    dst[0, :, c:c + LANES] = val.astype(BF16)
            if staged[0] is not None:
                planes, d_prev = staged, d

    xbc = _dot(u, wxbc_ref[...])
    dtr = jnp.pad(_dot(u, wdt_ref[...]), ((0, 0), (0, LANES - SSD_HEADS))) + dtb_ref[...]

    @pl.when(j == 0)
    def _():
        cbuf[0:CONV_HALO, :] = jnp.zeros((CONV_HALO, CONV_CH), F32)
        state[...] = jnp.zeros(state.shape, F32)

    cbuf[CONV_HALO:CONV_HALO + t, :] = xbc
    w = convw_ref[...]
    y = convb_ref[...] + w[CONV_WIDTH - 1:CONV_WIDTH] * xbc
    for s in range(1, CONV_WIDTH):
        y = y + w[CONV_WIDTH - 1 - s:CONV_WIDTH - s] * cbuf[CONV_HALO - s:CONV_HALO - s + t, :]
    cbuf[0:CONV_HALO, :] = cbuf[t:t + CONV_HALO, :]
    act = y * _sigmoid(y)
    xs_b = act[:, :SSD_WIDTH].astype(BF16)
    bm = act[:, SSD_WIDTH:SSD_WIDTH + SSD_STATE].astype(BF16)
    cm = act[:, SSD_WIDTH + SSD_STATE:].astype(BF16)

    dt = jnp.maximum(dtr, 0.0) + jnp.log1p(jnp.exp(-jnp.abs(dtr)))

    row = lax.broadcasted_iota(jnp.int32, (CHUNK, CHUNK), 0)
    col = lax.broadcasted_iota(jnp.int32, (CHUNK, CHUNK), 1)
    causal = row >= col
    low = col < SSD_HEAD_DIM
    tril = causal.astype(BF16)
    a_row = -jnp.exp(alog_ref[...]) * math.log2(math.e)
    a_col = -jnp.exp(alogc_ref[...]) * math.log2(math.e)
    groups = SSD_WIDTH // LANES
    st = [state[:, g * LANES:(g + 1) * LANES] for g in range(groups)]

    def ssd_step(c, st):
        rows = slice(c * CHUNK, (c + 1) * CHUNK)
        pre_c = _ssd_prepare(bm[rows], cm[rows], dt[rows], a_row, a_col, tril)
        yc, st = _ssd_chunk(xs_b[rows], pre_c, dskip_ref[...], st, causal, low)
        y_ref[0, rows, :] = yc
        return st

    lanes = [slice(g * LANES, (g + 1) * LANES) for g in range(ATTN_WIDTH // LANES)]
    assert t // CHUNK == 4
    q = _dot(u, wq_ref[...])
    st = ssd_step(0, st)
    for g, sl in enumerate(lanes):
        qs[g] = rope(q[:, sl]) * Q_SCALE
    k = _dot(u, wk_ref[...])
    emit_layouts(0)
    st = ssd_step(1, st)
    for g, sl in enumerate(lanes):
        ks[g] = rope(k[:, sl])
    v = _dot(u, wv_ref[...])
    emit_layouts(1)
    st = ssd_step(2, st)
    for g, sl in enumerate(lanes):
        vs[g] = v[:, sl]
    z_ref[0] = _dot(u, wz_ref[...]).astype(BF16)
    emit_layouts(2)
    st = ssd_step(3, st)
    for g in range(groups):
        state[:, g * LANES:(g + 1) * LANES] = st[g]


def _in_proj(x, pos3, invf, g, win, convw, convb, dtb, alog, alogc, dskip):
    b, s, _ = x.shape
    t = PROJ_TILE
    tile = lambda w_: pl.BlockSpec((1, t, w_), lambda bi, j: (bi, j, 0))
    full = lambda a: pl.BlockSpec(a.shape, lambda bi, j: (0,) * a.ndim)
    wide = jax.ShapeDtypeStruct((b, s, ATTN_WIDTH), BF16)
    qkv_specs, qkv_shapes = [], []
    for d in DILATIONS:
        qkv_specs += [pl.BlockSpec((1, t // d, d * ATTN_WIDTH), lambda bi, j: (bi, j, 0))] * 3
        qkv_shapes += [jax.ShapeDtypeStruct((b, s // d, d * ATTN_WIDTH), BF16)] * 3
    plane = pltpu.VMEM((ATTN_WIDTH // LANES, t, LANES), F32)
    return pl.pallas_call(
        _in_proj_kernel,
        grid=(b, s // t),
        in_specs=[tile(D_MODEL), pl.BlockSpec((1, 1, t), lambda bi, j: (bi, 0, j)),
                  full(invf), full(g), full(win), full(convw), full(convb), full(dtb), full(alog), full(alogc),
                  full(dskip)],
        out_specs=qkv_specs + [tile(ATTN_WIDTH), tile(SSD_WIDTH)],
        out_shape=qkv_shapes + [wide, wide],
        scratch_shapes=[pltpu.VMEM((t + CONV_HALO, CONV_CH), F32)]
                       + [plane] * (3 * (len(DILATIONS) - 1))
                       + [pltpu.VMEM((SSD_STATE, SSD_WIDTH), F32)],
        compiler_params=pltpu.CompilerParams(
            dimension_semantics=("arbitrary", "arbitrary"), vmem_limit_bytes=VMEM_LIMIT),
        name="in_proj",
    )(x, pos3, invf, g, win, convw, convb, dtb, alog, alogc, dskip)


def _attn_block(has_prev, span, q, kp, kc, vp, vc):
    blk = ATTN_BLOCK
    qi = lax.broadcasted_iota(jnp.int32, (blk, 2 * blk), 0)
    ki = lax.broadcasted_iota(jnp.int32, (blk, 2 * blk), 1)
    dist = qi + blk - ki
    valid = (dist >= 0) & (dist <= span) & ((ki >= blk) | has_prev)
    bias = jnp.where(valid, 0.0, -jnp.inf).astype(F32)
    bias = jnp.concatenate([bias, bias], axis=0)
    low = lax.broadcasted_iota(jnp.int32, (blk, LANES), 1) < HEAD_DIM

    k = jnp.concatenate([kp, kc], axis=0)
    v = jnp.concatenate([vp, vc], axis=0)
    outs, maxes, sums = [], [], []
    for g in range(ATTN_WIDTH // LANES):
        sl = slice(g * LANES, (g + 1) * LANES)
        q2, k2, v2 = q[:, sl], k[:, sl], v[:, sl]
        zero = jnp.zeros_like(q2)
        qq = jnp.concatenate([jnp.where(low, q2, zero), jnp.where(low, zero, q2)], axis=0)
        s = _dot_nt(qq, k2) + bias
        m = jnp.max(s, axis=-1, keepdims=True)
        p = jnp.exp2(s - m)
        l = jnp.sum(p, axis=-1, keepdims=True)
        o = _dot(p.astype(BF16), v2)
        outs.append(jnp.where(low, o[:blk], o[blk:]))
        maxes.append(jnp.where(low, m[:blk], m[blk:]))
        sums.append(jnp.where(low, l[:blk], l[blk:]))
    return outs, maxes, sums


def _attn_kernel(*refs):
    nbr = len(BRANCHES)
    in_refs = refs[:5 * nbr]
    out_ref = refs[5 * nbr]
    acc_o = refs[5 * nbr + 1:5 * nbr + 1 + nbr]
    acc_m = refs[5 * nbr + 1 + nbr:5 * nbr + 1 + 2 * nbr]
    acc_l = refs[5 * nbr + 1 + 2 * nbr:]
    group_steps = ATTN_GROUP // ATTN_BLOCK
    blk, wid = ATTN_BLOCK, ATTN_WIDTH

    for sub in range(ATTN_UNROLL):
        i = pl.program_id(1) * ATTN_UNROLL + sub
        local = i % group_steps
        for bi, (window, d) in enumerate(BRANCHES):
            q_ref, kp_ref, kc_ref, vp_ref, vc_ref = in_refs[5 * bi:5 * bi + 5]
            if d == 1:
                rows = slice(sub * blk, (sub + 1) * blk)
                before = slice((sub - 1) * blk, sub * blk)
                ops = (q_ref[0, rows, :],
                       kp_ref[0] if sub == 0 else kc_ref[0, before, :], kc_ref[0, rows, :],
                       vp_ref[0] if sub == 0 else vc_ref[0, before, :], vc_ref[0, rows, :])
                has_prev = (i > 0) if sub == 0 else True
            else:
                cols = slice(sub * wid, (sub + 1) * wid)
                ops = tuple(ref[0, :, cols] for ref in (q_ref, kp_ref, kc_ref, vp_ref, vc_ref))
                has_prev = (i // d) > 0
            o, m, l = _attn_block(has_prev, window // d, *ops)
            start = (local // d) * (d * blk) + i % d
            rows = pl.ds(start, blk) if d == 1 else pl.ds(start, blk, stride=d)
            for g in range(wid // LANES):
                acc_o[bi][g, rows, :] = o[g]
                acc_m[bi][g, rows, :] = m[g]
                acc_l[bi][g, rows, :] = l[g]

    @pl.when(local == group_steps - 1)
    def _():
        total = lambda xs: functools.reduce(lambda a, b_: a + b_, xs)

        def merge(c, carry):
            rows = pl.ds(pl.multiple_of(c * MERGE_ROWS, MERGE_ROWS), MERGE_ROWS)
            for g in range(ATTN_WIDTH // LANES):
                ms = [a[g, rows, :] for a in acc_m]
                top = functools.reduce(jnp.maximum, ms)
                es = [jnp.exp2(m - top) for m in ms]
                den = total([e * a[g, rows, :] for e, a in zip(es, acc_l)])
                num = total([e * a[g, rows, :] for e, a in zip(es, acc_o)])
                out_ref[0, rows, g * LANES:(g + 1) * LANES] = (num / den).astype(BF16)
            return carry

        lax.fori_loop(0, ATTN_GROUP // MERGE_ROWS, merge, 0)


def _attention(qkv):
    b = qkv[0].shape[0]
    s = qkv[0].shape[1] * DILATIONS[0]
    un = ATTN_UNROLL
    steps = s // (ATTN_BLOCK * un)
    group_steps = ATTN_GROUP // (ATTN_BLOCK * un)
    in_specs = []
    for d in DILATIONS:
        if d == 1:
            cur = pl.BlockSpec((1, un * ATTN_BLOCK, ATTN_WIDTH), lambda bi, j: (bi, j, 0))
            prev = pl.BlockSpec((1, ATTN_BLOCK, ATTN_WIDTH),
                                lambda bi, j: (bi, jnp.maximum(j * un - 1, 0), 0))
        else:
            assert d % un == 0
            cur = pl.BlockSpec((1, ATTN_BLOCK, un * ATTN_WIDTH),
                               lambda bi, j, d=d: (bi, (j * un) // d, (j * un % d) // un))
            prev = pl.BlockSpec((1, ATTN_BLOCK, un * ATTN_WIDTH),
                                lambda bi, j, d=d: (bi, jnp.maximum((j * un) // d - 1, 0),
                                                    (j * un % d) // un))
        in_specs += [cur, prev, cur, prev, cur]
    args = []
    for bi in range(len(DILATIONS)):
        q, k, v = qkv[3 * bi:3 * bi + 3]
        args += [q, k, k, v, v]
    nbr = len(BRANCHES)
    return pl.pallas_call(
        _attn_kernel,
        grid=(b, steps),
        in_specs=in_specs,
        out_specs=pl.BlockSpec((1, ATTN_GROUP, ATTN_WIDTH), lambda bi, i: (bi, i // group_steps, 0)),
        out_shape=jax.ShapeDtypeStruct((b, s, ATTN_WIDTH), BF16),
        scratch_shapes=[pltpu.VMEM((ATTN_WIDTH // LANES, ATTN_GROUP, LANES), F32)] * (3 * nbr),
        compiler_params=pltpu.CompilerParams(
            dimension_semantics=("arbitrary", "arbitrary"), vmem_limit_bytes=VMEM_LIMIT),
        name="dilated_attention",
    )(*args)


def _tail_kernel(x_ref, attn_ref, y_ref, z_ref, p_ref, gssd_ref, gmix_ref, wout_ref, gpre_ref,
                 wup_ref, wdown_ref, gmlp_ref, wgate_ref, wproj_ref, gple_ref, out_ref):
    def gate_ssd(d):
        zf = z_ref[d["rows"], :].astype(F32)
        y = y_ref[d["rows"], :].astype(F32) * (zf * _sigmoid(zf))
        d["y"] = (_rms(y) * gssd_ref[...]).astype(BF16)

    def out_proj(d):
        d["mix"] = (_dot(attn_ref[d["rows"], :], wout_ref[0:ATTN_WIDTH, :])
                    + _dot(d["y"], wout_ref[ATTN_WIDTH:, :]))

    def norm_mix(d):
        d["h"] = x_ref[d["rows"], :] + _rms(d["mix"]) * gmix_ref[...]
        d["u"] = (_rms(d["h"]) * gpre_ref[...]).astype(BF16)

    def mlp(d):
        ff = jnp.zeros(d["h"].shape, F32)
        for c in range(D_FF // FF_CHUNK):
            sl = slice(c * FF_CHUNK, (c + 1) * FF_CHUNK)
            hid = jnp.square(jnp.maximum(_dot(d["u"], wup_ref[:, sl]), 0.0)).astype(BF16)
            ff = ff + _dot(hid, wdown_ref[sl, :])
        d["ff"] = ff

    def norm_mlp(d):
        d["h"] = d["h"] + _rms(d["ff"]) * gmlp_ref[...]

    def ple_proj(d):
        d["gate"] = _dot(d["h"].astype(BF16), wgate_ref[...])
        d["ple"] = _dot(p_ref[d["rows"], :].astype(BF16), wproj_ref[...])

    def norm_ple(d):
        ple = d["ple"] * _sigmoid(d["gate"])
        out_ref[d["rows"], :] = d["h"] + _rms(ple) * gple_ref[...]

    half = x_ref.shape[0] // 2
    a = dict(rows=slice(0, half))
    b = dict(rows=slice(half, 2 * half))
    for stage, d in ((gate_ssd, a), (out_proj, a), (gate_ssd, b), (norm_mix, a), (out_proj, b),
                     (mlp, a), (norm_mix, b), (mlp, b), (norm_mlp, a), (ple_proj, a),
                     (norm_mlp, b), (norm_ple, a), (ple_proj, b), (norm_ple, b)):
        stage(d)


def _tail(x2, attn2, y2, z2, p2, gssd, gmix, wout, gpre, wup, wdown, gmlp, wgate, wproj, gple):
    n = x2.shape[0]
    t = TAIL_TILE
    tile = lambda w_: pl.BlockSpec((t, w_), lambda i: (i, 0))
    full = lambda a: pl.BlockSpec(a.shape, lambda i: (0, 0), pipeline_mode=pl.Buffered(1))
    return pl.pallas_call(
        _tail_kernel,
        grid=(n // t,),
        in_specs=[tile(D_MODEL), tile(ATTN_WIDTH), tile(SSD_WIDTH), tile(SSD_WIDTH),
                  tile(PLE_DIM), full(gssd), full(gmix),
                  full(wout), full(gpre), full(wup), full(wdown), full(gmlp), full(wgate),
                  full(wproj), full(gple)],
        out_specs=tile(D_MODEL),
        out_shape=jax.ShapeDtypeStruct((n, D_MODEL), F32),
        compiler_params=pltpu.CompilerParams(
            dimension_semantics=("arbitrary",), vmem_limit_bytes=VMEM_LIMIT),
        name="tail",
    )(x2, attn2, y2, z2, p2, gssd, gmix, wout, gpre, wup, wdown, gmlp, wgate, wproj, gple)


def _cast_kernel(*refs):
    n = len(refs) // 2
    for src, dst in zip(refs[:n], refs[n:]):
        dst[...] = src[...].astype(dst.dtype)


def _to_bf16(weights):
    slab = lambda a: pl.BlockSpec((a.shape[0] // CAST_STEPS, a.shape[1]), lambda i: (i, 0))
    return pl.pallas_call(
        _cast_kernel,
        grid=(CAST_STEPS,),
        in_specs=[slab(a) for a in weights],
        out_specs=[slab(a) for a in weights],
        out_shape=[jax.ShapeDtypeStruct(a.shape, BF16) for a in weights],
        compiler_params=pltpu.CompilerParams(
            dimension_semantics=("arbitrary",), vmem_limit_bytes=VMEM_LIMIT),
        name="weights_to_bf16",
    )(*weights)


def _pad_lanes(a):
    return jnp.pad(a, ((0, 0), (0, LANES - a.shape[-1])))


def kernel(x, p, positions, norm_mix_pre, norm_mix_post, w_in, conv_w, conv_b, dt_bias, a_log,
           d_skip, ssd_norm_g, w_out, norm_mlp_pre, norm_mlp_post, w_up, w_down, w_ple_gate,
           w_ple_proj, norm_ple_post):
    b, s, _ = x.shape
    half = HEAD_DIM // 2
    invf = (ROPE_THETA ** (-jnp.arange(half, dtype=F32) * 2.0 / HEAD_DIM)).reshape(half, 1)
    pos3 = positions.reshape(b, 1, s)
    row = lambda a: a.reshape(1, -1)

    h = x
    for i in range(w_in.shape[0]):
        win, wout, wup, wdown, wgate, wproj = _to_bf16(
            [w_in[i], w_out[i], w_up[i], w_down[i], w_ple_gate[i], w_ple_proj[i]])
        *qkv, z, y = _in_proj(
            h, pos3, invf, row(norm_mix_pre[i]), win,
            conv_w[i], row(conv_b[i]), _pad_lanes(row(dt_bias[i])), _pad_lanes(row(a_log[i])),
            jnp.pad(a_log[i], (0, SSD_HEADS)).reshape(2 * SSD_HEADS, 1),
            row(jnp.repeat(d_skip[i], SSD_HEAD_DIM)))
        attn = _attention(qkv)
        h = _tail(h.reshape(b * s, D_MODEL), attn.reshape(b * s, ATTN_WIDTH),
                  y.reshape(b * s, SSD_WIDTH), z.reshape(b * s, SSD_WIDTH),
                  p[i].reshape(b * s, PLE_DIM), row(ssd_norm_g[i]),
                  row(norm_mix_post[i]), wout, row(norm_mlp_pre[i]), wup, wdown,
                  row(norm_mlp_post[i]), wgate, wproj,
                  row(norm_ple_post[i])).reshape(b, s, D_MODEL)
    return h
```

```python
import functools
import math

import jax
import jax.numpy as jnp
from jax import lax
from jax.experimental import pallas as pl
from jax.experimental.pallas import tpu as pltpu

D_MODEL = 1024
PLE_DIM = 256
N_HEADS = 8
HEAD_DIM = 64
ATTN_WIDTH = N_HEADS * HEAD_DIM
SSD_HEADS = 8
SSD_HEAD_DIM = 64
SSD_WIDTH = SSD_HEADS * SSD_HEAD_DIM
SSD_STATE = 128
CONV_WIDTH = 4
CONV_CH = SSD_WIDTH + 2 * SSD_STATE
CHUNK = 128
D_FF = 4 * D_MODEL
ROPE_THETA = 10000.0
BRANCHES = ((128, 1), (512, 4), (2048, 16))
DILATIONS = tuple(d for _, d in BRANCHES)
ATTN_BLOCK = 128
NORM_EPS = 1e-6
Q_SCALE = HEAD_DIM ** -0.5 * math.log2(math.e)

LANES = 128
CONV_HALO = 8
PROJ_TILE = 512
TAIL_TILE = 512
FF_CHUNK = 1024
MERGE_ROWS = 256
ATTN_UNROLL = 4
ATTN_GROUP = ATTN_BLOCK * max(DILATIONS)
VMEM_LIMIT = 56 * 1024 * 1024

F32 = jnp.float32
BF16 = jnp.bfloat16


def _dot(a, b):
    return jnp.dot(a, b, preferred_element_type=F32)


def _dot_nt(a, b):
    return lax.dot_general(a, b, (((1,), (1,)), ((), ())), preferred_element_type=F32)


def _sigmoid(x):
    return 1.0 / (1.0 + jnp.exp(-x))


def _rms(x):
    return x * lax.rsqrt(jnp.mean(x * x, axis=-1, keepdims=True) + NORM_EPS)


def _split3(a):
    hi = a.astype(BF16)
    rest = a - hi.astype(F32)
    mid = rest.astype(BF16)
    return hi, mid, (rest - mid.astype(F32)).astype(BF16)


def _ssd_prepare(bm, cm, dt, a_row, a_col, tril):
    cs = sum(_dot(tril, piece) for piece in _split3(dt * a_row))
    dt_t = dt.T[0:2 * SSD_HEADS]
    cs_t = sum(_dot_nt(piece, tril) for piece in _split3(dt_t * a_col))
    w_t = dt_t * jnp.exp2(cs_t[:, CHUNK - 1:CHUNK] - cs_t)
    return dict(cs=cs, cs_dt_t=cs_t - jnp.log2(dt_t), w_t=w_t, cb=_dot_nt(cm, bm),
                cm_f=cm.astype(F32), bm_t=bm.astype(F32).T)


def _ssd_chunk(xs_b, pre, dskip, prev, causal, low):
    zero = jnp.zeros((CHUNK, LANES), BF16)
    ys, nxt = [], []
    for g in range(SSD_WIDTH // LANES):
        x2 = xs_b[:, g * LANES:(g + 1) * LANES]
        rhs = jnp.concatenate([x2, prev[g].astype(BF16)], axis=0)
        y_lhs, inj_lhs, dec2 = [], [], []
        for h in (2 * g, 2 * g + 1):
            cs_h = jnp.broadcast_to(pre["cs"][:, h:h + 1], (CHUNK, CHUNK))
            grow = jnp.exp2(cs_h)
            seg = jnp.exp2(jnp.where(causal, cs_h - pre["cs_dt_t"][h:h + 1, :], -jnp.inf))
            y_lhs.append(jnp.concatenate([pre["cb"] * seg, pre["cm_f"] * grow],
                                         axis=1).astype(BF16))
            inj_lhs.append((pre["bm_t"] * pre["w_t"][h:h + 1, :]).astype(BF16))
            dec2.append(grow[CHUNK - 1:CHUNK, :])
        y_pair = _dot(jnp.concatenate(y_lhs, axis=0), rhs)
        ys.append(jnp.where(low, y_pair[:CHUNK], y_pair[CHUNK:]))
        x_split = jnp.concatenate([jnp.where(low, x2, zero), jnp.where(low, zero, x2)], axis=0)
        inject = _dot(jnp.concatenate(inj_lhs, axis=1), x_split)
        nxt.append(prev[g] * jnp.where(low[0:1], dec2[0], dec2[1]) + inject)

    return (jnp.concatenate(ys, axis=1) + dskip * xs_b.astype(F32)).astype(BF16), nxt


def _in_proj_kernel(x_ref, pos_ref, invf_ref, g_ref, win_ref, convw_ref, convb_ref, dtb_ref,
                    alog_ref, alogc_ref, dskip_ref, *rest):
    widths = (ATTN_WIDTH, ATTN_WIDTH, ATTN_WIDTH, SSD_WIDTH, CONV_CH, SSD_HEADS)
    edges = [sum(widths[:c]) for c in range(len(widths) + 1)]
    wq_ref, wk_ref, wv_ref, wz_ref, wxbc_ref, wdt_ref = (
        win_ref.at[:, lo:hi] for lo, hi in zip(edges, edges[1:]))
    n_qkv = 3 * len(DILATIONS)
    qkv_refs = rest[:n_qkv]
    z_ref, y_ref, cbuf, qs, ks, vs = rest[n_qkv:n_qkv + 6]
    stages, state = rest[n_qkv + 6:-1], rest[-1]
    t = x_ref.shape[1]

    @pl.when(pl.program_id(1) == 0)
    def _():
        cbuf[0:CONV_HALO, :] = jnp.zeros((CONV_HALO, CONV_CH), F32)
        state[...] = jnp.zeros(state.shape, F32)

    u = (_rms(x_ref[0]) * g_ref[...]).astype(BF16)

    ang = invf_ref[...] * pos_ref[0].astype(F32)
    cos4 = jnp.concatenate([jnp.cos(ang)] * 4, axis=0).T
    sin4 = jnp.concatenate([jnp.sin(ang)] * 4, axis=0).T
    lane = lax.broadcasted_iota(jnp.int32, (t, LANES), 1)
    first_half = (lane % HEAD_DIM) < (HEAD_DIM // 2)
    sin_signed = jnp.where(first_half, -sin4, sin4)

    def rope(v):
        partner = jnp.where(first_half, pltpu.roll(v, LANES - HEAD_DIM // 2, 1),
                            pltpu.roll(v, HEAD_DIM // 2, 1))
        return v * cos4 + partner * sin_signed

    def emit_layouts(ti):
        planes, d_prev = (qs, ks, vs), 1
        for bi, d in enumerate(DILATIONS):
            f = d // d_prev
            staged = stages[3 * (bi - 1):3 * bi] if 0 < bi < len(DILATIONS) - 1 else (None,) * 3
            src, stage, dst = planes[ti], staged[ti], qkv_refs[3 * bi + ti]
            for r in range(d):
                start = (r % d_prev) * (t // d_prev) + r // d_prev
                for g in range(ATTN_WIDTH // LANES):
                    val = src[g] if d == 1 else src[g, pl.ds(start, t // d, stride=f), :]
                    if stage is not None:
                        stage[g, r * (t // d):(r + 1) * (t // d), :] = val
                    c = r * ATTN_WIDTH + g * LANES
                    dst[0, :, c:c + LANES] = val.astype(BF16)
            if staged[0] is not None:
                planes, d_prev = staged, d

    lanes = [slice(g * LANES, (g + 1) * LANES) for g in range(ATTN_WIDTH // LANES)]
    xbc = _dot(u, wxbc_ref[...])
    dtr = jnp.pad(_dot(u, wdt_ref[...]), ((0, 0), (0, LANES - SSD_HEADS))) + dtb_ref[...]
    q = _dot(u, wq_ref[...])

    cbuf[CONV_HALO:CONV_HALO + t, :] = xbc
    w = convw_ref[...]
    y = convb_ref[...] + w[CONV_WIDTH - 1:CONV_WIDTH] * xbc
    for s in range(1, CONV_WIDTH):
        y = y + w[CONV_WIDTH - 1 - s:CONV_WIDTH - s] * cbuf[CONV_HALO - s:CONV_HALO - s + t, :]
    cbuf[0:CONV_HALO, :] = cbuf[t:t + CONV_HALO, :]
    act = y * _sigmoid(y)
    xs_b = act[:, :SSD_WIDTH].astype(BF16)
    bm = act[:, SSD_WIDTH:SSD_WIDTH + SSD_STATE].astype(BF16)
    cm = act[:, SSD_WIDTH + SSD_STATE:].astype(BF16)

    dt = jnp.maximum(dtr, 0.0) + jnp.log1p(jnp.exp(-jnp.abs(dtr)))

    k = _dot(u, wk_ref[...])
    for g, sl in enumerate(lanes):
        qs[g] = rope(q[:, sl]) * Q_SCALE
    v = _dot(u, wv_ref[...])
    for g, sl in enumerate(lanes):
        ks[g] = rope(k[:, sl])

    row = lax.broadcasted_iota(jnp.int32, (CHUNK, CHUNK), 0)
    col = lax.broadcasted_iota(jnp.int32, (CHUNK, CHUNK), 1)
    causal = row >= col
    low = col < SSD_HEAD_DIM
    tril = causal.astype(BF16)
    a_row = -jnp.exp(alog_ref[...]) * math.log2(math.e)
    a_col = -jnp.exp(alogc_ref[...]) * math.log2(math.e)
    groups = SSD_WIDTH // LANES
    st = [state[:, g * LANES:(g + 1) * LANES] for g in range(groups)]

    def ssd_step(c, st):
        rows = slice(c * CHUNK, (c + 1) * CHUNK)
        pre_c = _ssd_prepare(bm[rows], cm[rows], dt[rows], a_row, a_col, tril)
        yc, st = _ssd_chunk(xs_b[rows], pre_c, dskip_ref[...], st, causal, low)
        y_ref[0, rows, :] = yc
        return st

    assert t // CHUNK == 4
    st = ssd_step(0, st)
    z_ref[0] = _dot(u, wz_ref[...]).astype(BF16)
    emit_layouts(0)
    st = ssd_step(1, st)
    emit_layouts(1)
    st = ssd_step(2, st)
    for g, sl in enumerate(lanes):
        vs[g] = v[:, sl]
    emit_layouts(2)
    st = ssd_step(3, st)
    for g in range(groups):
        state[:, g * LANES:(g + 1) * LANES] = st[g]


def _in_proj(x, pos3, invf, g, win, convw, convb, dtb, alog, alogc, dskip):
    b, s, _ = x.shape
    t = PROJ_TILE
    tile = lambda w_: pl.BlockSpec((1, t, w_), lambda bi, j: (bi, j, 0))
    full = lambda a: pl.BlockSpec(a.shape, lambda bi, j: (0,) * a.ndim)
    wide = jax.ShapeDtypeStruct((b, s, ATTN_WIDTH), BF16)
    qkv_specs, qkv_shapes = [], []
    for d in DILATIONS:
        qkv_specs += [pl.BlockSpec((1, t // d, d * ATTN_WIDTH), lambda bi, j: (bi, j, 0))] * 3
        qkv_shapes += [jax.ShapeDtypeStruct((b, s // d, d * ATTN_WIDTH), BF16)] * 3
    plane = pltpu.VMEM((ATTN_WIDTH // LANES, t, LANES), F32)
    return pl.pallas_call(
        _in_proj_kernel,
        grid=(b, s // t),
        in_specs=[tile(D_MODEL), pl.BlockSpec((1, 1, t), lambda bi, j: (bi, 0, j)),
                  full(invf), full(g), full(win), full(convw), full(convb), full(dtb), full(alog), full(alogc),
                  full(dskip)],
        out_specs=qkv_specs + [tile(ATTN_WIDTH), tile(SSD_WIDTH)],
        out_shape=qkv_shapes + [wide, wide],
        scratch_shapes=[pltpu.VMEM((t + CONV_HALO, CONV_CH), F32)]
                       + [plane] * (3 * (len(DILATIONS) - 1))
                       + [pltpu.VMEM((SSD_STATE, SSD_WIDTH), F32)],
        compiler_params=pltpu.CompilerParams(
            dimension_semantics=("arbitrary", "arbitrary"), vmem_limit_bytes=VMEM_LIMIT),
        name="in_proj",
    )(x, pos3, invf, g, win, convw, convb, dtb, alog, alogc, dskip)


def _attn_block(has_prev, span, q, kp, kc, vp, vc):
    blk = ATTN_BLOCK
    qi = lax.broadcasted_iota(jnp.int32, (blk, 2 * blk), 0)
    ki = lax.broadcasted_iota(jnp.int32, (blk, 2 * blk), 1)
    dist = qi + blk - ki
    valid = (dist >= 0) & (dist <= span) & ((ki >= blk) | has_prev)
    bias = jnp.where(valid, 0.0, -jnp.inf).astype(F32)
    bias = jnp.concatenate([bias, bias], axis=0)
    low = lax.broadcasted_iota(jnp.int32, (blk, LANES), 1) < HEAD_DIM

    k = jnp.concatenate([kp, kc], axis=0)
    v = jnp.concatenate([vp, vc], axis=0)
    outs, maxes, sums = [], [], []
    for g in range(ATTN_WIDTH // LANES):
        sl = slice(g * LANES, (g + 1) * LANES)
        q2, k2, v2 = q[:, sl], k[:, sl], v[:, sl]
        zero = jnp.zeros_like(q2)
        qq = jnp.concatenate([jnp.where(low, q2, zero), jnp.where(low, zero, q2)], axis=0)
        s = _dot_nt(qq, k2) + bias
        m = jnp.max(s, axis=-1, keepdims=True)
        p = jnp.exp2(s - m)
        l = jnp.sum(p, axis=-1, keepdims=True)
        o = _dot(p.astype(BF16), v2)
        outs.append(jnp.where(low, o[:blk], o[blk:]))
        maxes.append(jnp.where(low, m[:blk], m[blk:]))
        sums.append(jnp.where(low, l[:blk], l[blk:]))
    return outs, maxes, sums


def _attn_kernel(*refs):
    nbr = len(BRANCHES)
    in_refs = refs[:5 * nbr]
    out_ref = refs[5 * nbr]
    acc_o = refs[5 * nbr + 1:5 * nbr + 1 + nbr]
    acc_m = refs[5 * nbr + 1 + nbr:5 * nbr + 1 + 2 * nbr]
    acc_l = refs[5 * nbr + 1 + 2 * nbr:]
    group_steps = ATTN_GROUP // ATTN_BLOCK
    blk, wid = ATTN_BLOCK, ATTN_WIDTH

    for sub in range(ATTN_UNROLL):
        i = pl.program_id(1) * ATTN_UNROLL + sub
        local = i % group_steps
        for bi, (window, d) in enumerate(BRANCHES):
            q_ref, kp_ref, kc_ref, vp_ref, vc_ref = in_refs[5 * bi:5 * bi + 5]
            if d == 1:
                rows = slice(sub * blk, (sub + 1) * blk)
                before = slice((sub - 1) * blk, sub * blk)
                ops = (q_ref[0, rows, :],
                       kp_ref[0] if sub == 0 else kc_ref[0, before, :], kc_ref[0, rows, :],
                       vp_ref[0] if sub == 0 else vc_ref[0, before, :], vc_ref[0, rows, :])
                has_prev = (i > 0) if sub == 0 else True
            else:
                cols = slice(sub * wid, (sub + 1) * wid)
                ops = tuple(ref[0, :, cols] for ref in (q_ref, kp_ref, kc_ref, vp_ref, vc_ref))
                has_prev = (i // d) > 0
            o, m, l = _attn_block(has_prev, window // d, *ops)
            start = (local // d) * (d * blk) + i % d
            rows = pl.ds(start, blk) if d == 1 else pl.ds(start, blk, stride=d)
            for g in range(wid // LANES):
                acc_o[bi][g, rows, :] = o[g]
                acc_m[bi][g, rows, :] = m[g]
                acc_l[bi][g, rows, :] = l[g]

    @pl.when(local == group_steps - 1)
    def _():
        total = lambda xs: functools.reduce(lambda a, b_: a + b_, xs)

        def merge(c, carry):
            rows = pl.ds(pl.multiple_of(c * MERGE_ROWS, MERGE_ROWS), MERGE_ROWS)
            for g in range(ATTN_WIDTH // LANES):
                ms = [a[g, rows, :] for a in acc_m]
                top = functools.reduce(jnp.maximum, ms)
                es = [jnp.exp2(m - top) for m in ms]
                den = total([e * a[g, rows, :] for e, a in zip(es, acc_l)])
                num = total([e * a[g, rows, :] for e, a in zip(es, acc_o)])
                out_ref[0, rows, g * LANES:(g + 1) * LANES] = (num / den).astype(BF16)
            return carry

        lax.fori_loop(0, ATTN_GROUP // MERGE_ROWS, merge, 0)


def _attention(qkv):
    b = qkv[0].shape[0]
    s = qkv[0].shape[1] * DILATIONS[0]
    un = ATTN_UNROLL
    steps = s // (ATTN_BLOCK * un)
    group_steps = ATTN_GROUP // (ATTN_BLOCK * un)
    in_specs = []
    for d in DILATIONS:
        if d == 1:
            cur = pl.BlockSpec((1, un * ATTN_BLOCK, ATTN_WIDTH), lambda bi, j: (bi, j, 0))
            prev = pl.BlockSpec((1, ATTN_BLOCK, ATTN_WIDTH),
                                lambda bi, j: (bi, jnp.maximum(j * un - 1, 0), 0))
        else:
            assert d % un == 0
            cur = pl.BlockSpec((1, ATTN_BLOCK, un * ATTN_WIDTH),
                               lambda bi, j, d=d: (bi, (j * un) // d, (j * un % d) // un))
            prev = pl.BlockSpec((1, ATTN_BLOCK, un * ATTN_WIDTH),
                                lambda bi, j, d=d: (bi, jnp.maximum((j * un) // d - 1, 0),
                                                    (j * un % d) // un))
        in_specs += [cur, prev, cur, prev, cur]
    args = []
    for bi in range(len(DILATIONS)):
        q, k, v = qkv[3 * bi:3 * bi + 3]
        args += [q, k, k, v, v]
    nbr = len(BRANCHES)
    return pl.pallas_call(
        _attn_kernel,
        grid=(b, steps),
        in_specs=in_specs,
        out_specs=pl.BlockSpec((1, ATTN_GROUP, ATTN_WIDTH), lambda bi, i: (bi, i // group_steps, 0)),
        out_shape=jax.ShapeDtypeStruct((b, s, ATTN_WIDTH), BF16),
        scratch_shapes=[pltpu.VMEM((ATTN_WIDTH // LANES, ATTN_GROUP, LANES), F32)] * (3 * nbr),
        compiler_params=pltpu.CompilerParams(
            dimension_semantics=("arbitrary", "arbitrary"), vmem_limit_bytes=VMEM_LIMIT),
        name="dilated_attention",
    )(*args)


def _tail_kernel(x_ref, attn_ref, y_ref, z_ref, p_ref, gssd_ref, gmix_ref, wout_ref, gpre_ref,
                 wup_ref, wdown_ref, gmlp_ref, wgate_ref, wproj_ref, gple_ref, out_ref):
    def gate_ssd(d):
        zf = z_ref[d["rows"], :].astype(F32)
        y = y_ref[d["rows"], :].astype(F32) * (zf * _sigmoid(zf))
        d["y"] = (_rms(y) * gssd_ref[...]).astype(BF16)

    def out_proj(d):
        d["mix"] = (_dot(attn_ref[d["rows"], :], wout_ref[0:ATTN_WIDTH, :])
                    + _dot(d["y"], wout_ref[ATTN_WIDTH:, :]))

    def norm_mix(d):
        d["h"] = x_ref[d["rows"], :] + _rms(d["mix"]) * gmix_ref[...]
        d["u"] = (_rms(d["h"]) * gpre_ref[...]).astype(BF16)

    def mlp(d):
        ff = jnp.zeros(d["h"].shape, F32)
        for c in range(D_FF // FF_CHUNK):
            sl = slice(c * FF_CHUNK, (c + 1) * FF_CHUNK)
            hid = jnp.square(jnp.maximum(_dot(d["u"], wup_ref[:, sl]), 0.0)).astype(BF16)
            ff = ff + _dot(hid, wdown_ref[sl, :])
        d["ff"] = ff

    def norm_mlp(d):
        d["h"] = d["h"] + _rms(d["ff"]) * gmlp_ref[...]

    def ple_proj(d):
        d["gate"] = _dot(d["h"].astype(BF16), wgate_ref[...])
        d["ple"] = _dot(p_ref[d["rows"], :].astype(BF16), wproj_ref[...])

    def norm_ple(d):
        ple = d["ple"] * _sigmoid(d["gate"])
        out_ref[d["rows"], :] = d["h"] + _rms(ple) * gple_ref[...]

    half = x_ref.shape[0] // 2
    a = dict(rows=slice(0, half))
    b = dict(rows=slice(half, 2 * half))
    for stage, d in ((gate_ssd, a), (out_proj, a), (gate_ssd, b), (norm_mix, a), (out_proj, b),
                     (mlp, a), (norm_mix, b), (mlp, b), (norm_mlp, a), (ple_proj, a),
                     (norm_mlp, b), (norm_ple, a), (ple_proj, b), (norm_ple, b)):
        stage(d)


def _tail(x2, attn2, y2, z2, p2, gssd, gmix, wout, gpre, wup, wdown, gmlp, wgate, wproj, gple):
    n = x2.shape[0]
    t = TAIL_TILE
    tile = lambda w_: pl.BlockSpec((t, w_), lambda i: (i, 0))
    full = lambda a: pl.BlockSpec(a.shape, lambda i: (0, 0), pipeline_mode=pl.Buffered(1))
    return pl.pallas_call(
        _tail_kernel,
        grid=(n // t,),
        in_specs=[tile(D_MODEL), tile(ATTN_WIDTH), tile(SSD_WIDTH), tile(SSD_WIDTH),
                  tile(PLE_DIM), full(gssd), full(gmix),
                  full(wout), full(gpre), full(wup), full(wdown), full(gmlp), full(wgate),
                  full(wproj), full(gple)],
        out_specs=tile(D_MODEL),
        out_shape=jax.ShapeDtypeStruct((n, D_MODEL), F32),
        compiler_params=pltpu.CompilerParams(
            dimension_semantics=("arbitrary",), vmem_limit_bytes=VMEM_LIMIT),
        name="tail",
    )(x2, attn2, y2, z2, p2, gssd, gmix, wout, gpre, wup, wdown, gmlp, wgate, wproj, gple)


def _pad_lanes(a):
    return jnp.pad(a, ((0, 0), (0, LANES - a.shape[-1])))


def kernel(x, p, positions, norm_mix_pre, norm_mix_post, w_in, conv_w, conv_b, dt_bias, a_log,
           d_skip, ssd_norm_g, w_out, norm_mlp_pre, norm_mlp_post, w_up, w_down, w_ple_gate,
           w_ple_proj, norm_ple_post):
    b, s, _ = x.shape
    half = HEAD_DIM // 2
    invf = (ROPE_THETA ** (-jnp.arange(half, dtype=F32) * 2.0 / HEAD_DIM)).reshape(half, 1)
    pos3 = positions.reshape(b, 1, s)
    row = lambda a: a.reshape(1, -1)

    h = x
    for i in range(w_in.shape[0]):
        *qkv, z, y = _in_proj(
            h, pos3, invf, row(norm_mix_pre[i]), w_in[i].astype(BF16),
            conv_w[i], row(conv_b[i]), _pad_lanes(row(dt_bias[i])), _pad_lanes(row(a_log[i])),
            jnp.pad(a_log[i], (0, SSD_HEADS)).reshape(2 * SSD_HEADS, 1),
            row(jnp.repeat(d_skip[i], SSD_HEAD_DIM)))
        attn = _attention(qkv)
        h = _tail(h.reshape(b * s, D_MODEL), attn.reshape(b * s, ATTN_WIDTH),
                  y.reshape(b * s, SSD_WIDTH), z.reshape(b * s, SSD_WIDTH),
                  p[i].reshape(b * s, PLE_DIM), row(ssd_norm_g[i]),
                  row(norm_mix_post[i]), w_out[i].astype(BF16), row(norm_mlp_pre[i]),
                  w_up[i].astype(BF16), w_down[i].astype(BF16), row(norm_mlp_post[i]),
                  w_ple_gate[i].astype(BF16), w_ple_proj[i].astype(BF16),
                  row(norm_ple_post[i])).reshape(b, s, D_MODEL)
    return h
```

```python
import functools
import math

import jax
import jax.numpy as jnp
from jax import lax
from jax.experimental import pallas as pl
from jax.experimental.pallas import tpu as pltpu

D_MODEL = 1024
PLE_DIM = 256
N_HEADS = 8
HEAD_DIM = 64
ATTN_WIDTH = N_HEADS * HEAD_DIM
SSD_HEADS = 8
SSD_HEAD_DIM = 64
SSD_WIDTH = SSD_HEADS * SSD_HEAD_DIM
SSD_STATE = 128
CONV_WIDTH = 4
CONV_CH = SSD_WIDTH + 2 * SSD_STATE
CHUNK = 128
D_FF = 4 * D_MODEL
ROPE_THETA = 10000.0
BRANCHES = ((128, 1), (512, 4), (2048, 16))
DILATIONS = tuple(d for _, d in BRANCHES)
ATTN_BLOCK = 128
NORM_EPS = 1e-6
Q_SCALE = HEAD_DIM ** -0.5 * math.log2(math.e)

LANES = 128
CONV_HALO = 8
PROJ_TILE = 512
TAIL_TILE = 512
FF_CHUNK = 1024
MERGE_ROWS = 256
ATTN_UNROLL = 4
ATTN_GROUP = ATTN_BLOCK * max(DILATIONS)
VMEM_LIMIT = 56 * 1024 * 1024

F32 = jnp.float32
BF16 = jnp.bfloat16


def _dot(a, b):
    return jnp.dot(a, b, preferred_element_type=F32)


def _dot_nt(a, b):
    return lax.dot_general(a, b, (((1,), (1,)), ((), ())), preferred_element_type=F32)


def _sigmoid(x):
    return 1.0 / (1.0 + jnp.exp(-x))


def _rms(x):
    return x * lax.rsqrt(jnp.mean(x * x, axis=-1, keepdims=True) + NORM_EPS)


def _split3(a):
    hi = a.astype(BF16)
    rest = a - hi.astype(F32)
    mid = rest.astype(BF16)
    return hi, mid, (rest - mid.astype(F32)).astype(BF16)


def _ssd_prepare(bm, cm, dt, a_row, a_col, tril):
    cs = sum(_dot(tril, piece) for piece in _split3(dt * a_row))
    dt_t = dt.T[0:2 * SSD_HEADS]
    cs_t = sum(_dot_nt(piece, tril) for piece in _split3(dt_t * a_col))
    w_t = dt_t * jnp.exp2(cs_t[:, CHUNK - 1:CHUNK] - cs_t)
    return dict(cs=cs, cs_dt_t=cs_t - jnp.log2(dt_t), w_t=w_t, cb=_dot_nt(cm, bm),
                cm_f=cm.astype(F32), bm_t=bm.astype(F32).T)


def _ssd_chunk(xs_b, pre, dskip, prev, causal, low):
    zero = jnp.zeros((CHUNK, LANES), BF16)
    ys, nxt = [], []
    for g in range(SSD_WIDTH // LANES):
        x2 = xs_b[:, g * LANES:(g + 1) * LANES]
        rhs = jnp.concatenate([x2, prev[g].astype(BF16)], axis=0)
        y_lhs, inj_lhs, dec2 = [], [], []
        for h in (2 * g, 2 * g + 1):
            cs_h = jnp.broadcast_to(pre["cs"][:, h:h + 1], (CHUNK, CHUNK))
            grow = jnp.exp2(cs_h)
            seg = jnp.exp2(jnp.where(causal, cs_h - pre["cs_dt_t"][h:h + 1, :], -jnp.inf))
            y_lhs.append(jnp.concatenate([pre["cb"] * seg, pre["cm_f"] * grow],
                                         axis=1).astype(BF16))
            inj_lhs.append((pre["bm_t"] * pre["w_t"][h:h + 1, :]).astype(BF16))
            dec2.append(grow[CHUNK - 1:CHUNK, :])
        y_pair = _dot(jnp.concatenate(y_lhs, axis=0), rhs)
        ys.append(jnp.where(low, y_pair[:CHUNK], y_pair[CHUNK:]))
        x_split = jnp.concatenate([jnp.where(low, x2, zero), jnp.where(low, zero, x2)], axis=0)
        inject = _dot(jnp.concatenate(inj_lhs, axis=1), x_split)
        nxt.append(prev[g] * jnp.where(low[0:1], dec2[0], dec2[1]) + inject)

    return (jnp.concatenate(ys, axis=1) + dskip * xs_b.astype(F32)).astype(BF16), nxt


def _in_proj_kernel(x_ref, pos_ref, invf_ref, g_ref, win_ref, convw_ref, convb_ref, dtb_ref,
                    alog_ref, alogc_ref, dskip_ref, *rest):
    widths = (ATTN_WIDTH, ATTN_WIDTH, ATTN_WIDTH, SSD_WIDTH, CONV_CH, SSD_HEADS)
    edges = [sum(widths[:c]) for c in range(len(widths) + 1)]
    wq_ref, wk_ref, wv_ref, wz_ref, wxbc_ref, wdt_ref = (
        win_ref.at[:, lo:hi] for lo, hi in zip(edges, edges[1:]))
    n_qkv = 3 * len(DILATIONS)
    qkv_refs = rest[:n_qkv]
    z_ref, y_ref, cbuf, qs, ks, vs = rest[n_qkv:n_qkv + 6]
    stages, state = rest[n_qkv + 6:-1], rest[-1]
    t = x_ref.shape[1]

    @pl.when(pl.program_id(1) == 0)
    def _():
        cbuf[0:CONV_HALO, :] = jnp.zeros((CONV_HALO, CONV_CH), F32)
        state[...] = jnp.zeros(state.shape, F32)

    u = (_rms(x_ref[0]) * g_ref[...]).astype(BF16)

    ang = invf_ref[...] * pos_ref[0].astype(F32)
    cos4 = jnp.concatenate([jnp.cos(ang)] * 4, axis=0).T
    sin4 = jnp.concatenate([jnp.sin(ang)] * 4, axis=0).T
    lane = lax.broadcasted_iota(jnp.int32, (t, LANES), 1)
    first_half = (lane % HEAD_DIM) < (HEAD_DIM // 2)
    sin_signed = jnp.where(first_half, -sin4, sin4)

    def rope(v):
        partner = jnp.where(first_half, pltpu.roll(v, LANES - HEAD_DIM // 2, 1),
                            pltpu.roll(v, HEAD_DIM // 2, 1))
        return v * cos4 + partner * sin_signed

    def emit_layouts(ti):
        planes, d_prev = (qs, ks, vs), 1
        for bi, d in enumerate(DILATIONS):
            f = d // d_prev
            staged = stages[3 * (bi - 1):3 * bi] if 0 < bi < len(DILATIONS) - 1 else (None,) * 3
            src, stage, dst = planes[ti], staged[ti], qkv_refs[3 * bi + ti]
            for r in range(d):
                start = (r % d_prev) * (t // d_prev) + r // d_prev
                for g in range(ATTN_WIDTH // LANES):
                    val = src[g] if d == 1 else src[g, pl.ds(start, t // d, stride=f), :]
                    if stage is not None:
                        stage[g, r * (t // d):(r + 1) * (t // d), :] = val
                    c = r * ATTN_WIDTH + g * LANES
                    dst[0, :, c:c + LANES] = val.astype(BF16)
            if staged[0] is not None:
                planes, d_prev = staged, d

    lanes = [slice(g * LANES, (g + 1) * LANES) for g in range(ATTN_WIDTH // LANES)]
    xbc = _dot(u, wxbc_ref[...])
    dtr = jnp.pad(_dot(u, wdt_ref[...]), ((0, 0), (0, LANES - SSD_HEADS))) + dtb_ref[...]
    q = _dot(u, wq_ref[...])

    cbuf[CONV_HALO:CONV_HALO + t, :] = xbc
    w = convw_ref[...]
    y = convb_ref[...] + w[CONV_WIDTH - 1:CONV_WIDTH] * xbc
    for s in range(1, CONV_WIDTH):
        y = y + w[CONV_WIDTH - 1 - s:CONV_WIDTH - s] * cbuf[CONV_HALO - s:CONV_HALO - s + t, :]
    cbuf[0:CONV_HALO, :] = cbuf[t:t + CONV_HALO, :]
    act = y * _sigmoid(y)
    xs_b = act[:, :SSD_WIDTH].astype(BF16)
    bm = act[:, SSD_WIDTH:SSD_WIDTH + SSD_STATE].astype(BF16)
    cm = act[:, SSD_WIDTH + SSD_STATE:].astype(BF16)

    dt = jnp.maximum(dtr, 0.0) + jnp.log1p(jnp.exp(-jnp.abs(dtr)))

    k = _dot(u, wk_ref[...])
    for g, sl in enumerate(lanes):
        qs[g] = rope(q[:, sl]) * Q_SCALE
    v = _dot(u, wv_ref[...])
    for g, sl in enumerate(lanes):
        ks[g] = rope(k[:, sl])

    row = lax.broadcasted_iota(jnp.int32, (CHUNK, CHUNK), 0)
    col = lax.broadcasted_iota(jnp.int32, (CHUNK, CHUNK), 1)
    causal = row >= col
    low = col < SSD_HEAD_DIM
    tril = causal.astype(BF16)
    a_row = -jnp.exp(alog_ref[...]) * math.log2(math.e)
    a_col = -jnp.exp(alogc_ref[...]) * math.log2(math.e)
    groups = SSD_WIDTH // LANES
    st = [state[:, g * LANES:(g + 1) * LANES] for g in range(groups)]

    chunk_rows = [slice(c * CHUNK, (c + 1) * CHUNK) for c in range(t // CHUNK)]
    pres = [_ssd_prepare(bm[rows], cm[rows], dt[rows], a_row, a_col, tril) for rows in chunk_rows]

    def ssd_step(c, st):
        rows = chunk_rows[c]
        yc, st = _ssd_chunk(xs_b[rows], pres[c], dskip_ref[...], st, causal, low)
        y_ref[0, rows, :] = yc
        return st

    assert t // CHUNK == 4
    st = ssd_step(0, st)
    z_ref[0] = _dot(u, wz_ref[...]).astype(BF16)
    emit_layouts(0)
    st = ssd_step(1, st)
    emit_layouts(1)
    st = ssd_step(2, st)
    for g, sl in enumerate(lanes):
        vs[g] = v[:, sl]
    emit_layouts(2)
    st = ssd_step(3, st)
    for g in range(groups):
        state[:, g * LANES:(g + 1) * LANES] = st[g]


def _in_proj(x, pos3, invf, g, win, convw, convb, dtb, alog, alogc, dskip):
    b, s, _ = x.shape
    t = PROJ_TILE
    tile = lambda w_: pl.BlockSpec((1, t, w_), lambda bi, j: (bi, j, 0))
    full = lambda a: pl.BlockSpec(a.shape, lambda bi, j: (0,) * a.ndim)
    wide = jax.ShapeDtypeStruct((b, s, ATTN_WIDTH), BF16)
    qkv_specs, qkv_shapes = [], []
    for d in DILATIONS:
        qkv_specs += [pl.BlockSpec((1, t // d, d * ATTN_WIDTH), lambda bi, j: (bi, j, 0))] * 3
        qkv_shapes += [jax.ShapeDtypeStruct((b, s // d, d * ATTN_WIDTH), BF16)] * 3
    plane = pltpu.VMEM((ATTN_WIDTH // LANES, t, LANES), F32)
    return pl.pallas_call(
        _in_proj_kernel,
        grid=(b, s // t),
        in_specs=[tile(D_MODEL), pl.BlockSpec((1, 1, t), lambda bi, j: (bi, 0, j)),
                  full(invf), full(g), full(win), full(convw), full(convb), full(dtb), full(alog), full(alogc),
                  full(dskip)],
        out_specs=qkv_specs + [tile(ATTN_WIDTH), tile(SSD_WIDTH)],
        out_shape=qkv_shapes + [wide, wide],
        scratch_shapes=[pltpu.VMEM((t + CONV_HALO, CONV_CH), F32)]
                       + [plane] * (3 * (len(DILATIONS) - 1))
                       + [pltpu.VMEM((SSD_STATE, SSD_WIDTH), F32)],
        compiler_params=pltpu.CompilerParams(
            dimension_semantics=("arbitrary", "arbitrary"), vmem_limit_bytes=VMEM_LIMIT),
        name="in_proj",
    )(x, pos3, invf, g, win, convw, convb, dtb, alog, alogc, dskip)


def _attn_block(has_prev, span, q, kp, kc, vp, vc):
    blk = ATTN_BLOCK
    qi = lax.broadcasted_iota(jnp.int32, (blk, 2 * blk), 0)
    ki = lax.broadcasted_iota(jnp.int32, (blk, 2 * blk), 1)
    dist = qi + blk - ki
    valid = (dist >= 0) & (dist <= span) & ((ki >= blk) | has_prev)
    bias = jnp.where(valid, 0.0, -jnp.inf).astype(F32)
    bias = jnp.concatenate([bias, bias], axis=0)
    low = lax.broadcasted_iota(jnp.int32, (blk, LANES), 1) < HEAD_DIM

    k = jnp.concatenate([kp, kc], axis=0)
    v = jnp.concatenate([vp, vc], axis=0)
    outs, maxes, sums = [], [], []
    for g in range(ATTN_WIDTH // LANES):
        sl = slice(g * LANES, (g + 1) * LANES)
        q2, k2, v2 = q[:, sl], k[:, sl], v[:, sl]
        zero = jnp.zeros_like(q2)
        qq = jnp.concatenate([jnp.where(low, q2, zero), jnp.where(low, zero, q2)], axis=0)
        s = _dot_nt(qq, k2) + bias
        m = jnp.max(s, axis=-1, keepdims=True)
        p = jnp.exp2(s - m)
        l = jnp.sum(p, axis=-1, keepdims=True)
        o = _dot(p.astype(BF16), v2)
        outs.append(jnp.where(low, o[:blk], o[blk:]))
        maxes.append(jnp.where(low, m[:blk], m[blk:]))
        sums.append(jnp.where(low, l[:blk], l[blk:]))
    return outs, maxes, sums


def _attn_kernel(*refs):
    nbr = len(BRANCHES)
    in_refs = refs[:5 * nbr]
    out_ref = refs[5 * nbr]
    acc_o = refs[5 * nbr + 1:5 * nbr + 1 + nbr]
    acc_m = refs[5 * nbr + 1 + nbr:5 * nbr + 1 + 2 * nbr]
    acc_l = refs[5 * nbr + 1 + 2 * nbr:]
    group_steps = ATTN_GROUP // ATTN_BLOCK
    blk, wid = ATTN_BLOCK, ATTN_WIDTH

    for sub in range(ATTN_UNROLL):
        i = pl.program_id(1) * ATTN_UNROLL + sub
        local = i % group_steps
        for bi, (window, d) in enumerate(BRANCHES):
            q_ref, kp_ref, kc_ref, vp_ref, vc_ref = in_refs[5 * bi:5 * bi + 5]
            if d == 1:
                rows = slice(sub * blk, (sub + 1) * blk)
                before = slice((sub - 1) * blk, sub * blk)
                ops = (q_ref[0, rows, :],
                       kp_ref[0] if sub == 0 else kc_ref[0, before, :], kc_ref[0, rows, :],
                       vp_ref[0] if sub == 0 else vc_ref[0, before, :], vc_ref[0, rows, :])
                has_prev = (i > 0) if sub == 0 else True
            else:
                cols = slice(sub * wid, (sub + 1) * wid)
                ops = tuple(ref[0, :, cols] for ref in (q_ref, kp_ref, kc_ref, vp_ref, vc_ref))
                has_prev = (i // d) > 0
            o, m, l = _attn_block(has_prev, window // d, *ops)
            start = (local // d) * (d * blk) + i % d
            rows = pl.ds(start, blk) if d == 1 else pl.ds(start, blk, stride=d)
            for g in range(wid // LANES):
                acc_o[bi][g, rows, :] = o[g]
                acc_m[bi][g, rows, :] = m[g]
                acc_l[bi][g, rows, :] = l[g]

    @pl.when(local == group_steps - 1)
    def _():
        total = lambda xs: functools.reduce(lambda a, b_: a + b_, xs)

        def merge(c, carry):
            rows = pl.ds(pl.multiple_of(c * MERGE_ROWS, MERGE_ROWS), MERGE_ROWS)
            for g in range(ATTN_WIDTH // LANES):
                ms = [a[g, rows, :] for a in acc_m]
                top = functools.reduce(jnp.maximum, ms)
                es = [jnp.exp2(m - top) for m in ms]
                den = total([e * a[g, rows, :] for e, a in zip(es, acc_l)])
                num = total([e * a[g, rows, :] for e, a in zip(es, acc_o)])
                out_ref[0, rows, g * LANES:(g + 1) * LANES] = (num / den).astype(BF16)
            return carry

        lax.fori_loop(0, ATTN_GROUP // MERGE_ROWS, merge, 0)


def _attention(qkv):
    b = qkv[0].shape[0]
    s = qkv[0].shape[1] * DILATIONS[0]
    un = ATTN_UNROLL
    steps = s // (ATTN_BLOCK * un)
    group_steps = ATTN_GROUP // (ATTN_BLOCK * un)
    in_specs = []
    for d in DILATIONS:
        if d == 1:
            cur = pl.BlockSpec((1, un * ATTN_BLOCK, ATTN_WIDTH), lambda bi, j: (bi, j, 0))
            prev = pl.BlockSpec((1, ATTN_BLOCK, ATTN_WIDTH),
                                lambda bi, j: (bi, jnp.maximum(j * un - 1, 0), 0))
        else:
            assert d % un == 0
            cur = pl.BlockSpec((1, ATTN_BLOCK, un * ATTN_WIDTH),
                               lambda bi, j, d=d: (bi, (j * un) // d, (j * un % d) // un))
            prev = pl.BlockSpec((1, ATTN_BLOCK, un * ATTN_WIDTH),
                                lambda bi, j, d=d: (bi, jnp.maximum((j * un) // d - 1, 0),
                                                    (j * un % d) // un))
        in_specs += [cur, prev, cur, prev, cur]
    args = []
    for bi in range(len(DILATIONS)):
        q, k, v = qkv[3 * bi:3 * bi + 3]
        args += [q, k, k, v, v]
    nbr = len(BRANCHES)
    return pl.pallas_call(
        _attn_kernel,
        grid=(b, steps),
        in_specs=in_specs,
        out_specs=pl.BlockSpec((1, ATTN_GROUP, ATTN_WIDTH), lambda bi, i: (bi, i // group_steps, 0)),
        out_shape=jax.ShapeDtypeStruct((b, s, ATTN_WIDTH), BF16),
        scratch_shapes=[pltpu.VMEM((ATTN_WIDTH // LANES, ATTN_GROUP, LANES), F32)] * (3 * nbr),
        compiler_params=pltpu.CompilerParams(
            dimension_semantics=("arbitrary", "arbitrary"), vmem_limit_bytes=VMEM_LIMIT),
        name="dilated_attention",
    )(*args)


def _tail_kernel(x_ref, attn_ref, y_ref, z_ref, p_ref, gssd_ref, gmix_ref, wout_ref, gpre_ref,
                 wup_ref, wdown_ref, gmlp_ref, wgate_ref, wproj_ref, gple_ref, out_ref):
    def gate_ssd(d):
        zf = z_ref[d["rows"], :].astype(F32)
        y = y_ref[d["rows"], :].astype(F32) * (zf * _sigmoid(zf))
        d["y"] = (_rms(y) * gssd_ref[...]).astype(BF16)

    def out_proj(d):
        d["mix"] = (_dot(attn_ref[d["rows"], :], wout_ref[0:ATTN_WIDTH, :])
                    + _dot(d["y"], wout_ref[ATTN_WIDTH:, :]))

    def norm_mix(d):
        d["h"] = x_ref[d["rows"], :] + _rms(d["mix"]) * gmix_ref[...]
        d["u"] = (_rms(d["h"]) * gpre_ref[...]).astype(BF16)

    def mlp(d):
        ff = jnp.zeros(d["h"].shape, F32)
        for c in range(D_FF // FF_CHUNK):
            sl = slice(c * FF_CHUNK, (c + 1) * FF_CHUNK)
            hid = jnp.square(jnp.maximum(_dot(d["u"], wup_ref[:, sl]), 0.0)).astype(BF16)
            ff = ff + _dot(hid, wdown_ref[sl, :])
        d["ff"] = ff

    def norm_mlp(d):
        d["h"] = d["h"] + _rms(d["ff"]) * gmlp_ref[...]

    def ple_proj(d):
        d["gate"] = _dot(d["h"].astype(BF16), wgate_ref[...])
        d["ple"] = _dot(p_ref[d["rows"], :].astype(BF16), wproj_ref[...])

    def norm_ple(d):
        ple = d["ple"] * _sigmoid(d["gate"])
        out_ref[d["rows"], :] = d["h"] + _rms(ple) * gple_ref[...]

    half = x_ref.shape[0] // 2
    a = dict(rows=slice(0, half))
    b = dict(rows=slice(half, 2 * half))
    for stage, d in ((gate_ssd, a), (out_proj, a), (gate_ssd, b), (norm_mix, a), (out_proj, b),
                     (mlp, a), (norm_mix, b), (mlp, b), (norm_mlp, a), (ple_proj, a),
                     (norm_mlp, b), (norm_ple, a), (ple_proj, b), (norm_ple, b)):
        stage(d)


def _tail(x2, attn2, y2, z2, p2, gssd, gmix, wout, gpre, wup, wdown, gmlp, wgate, wproj, gple):
    n = x2.shape[0]
    t = TAIL_TILE
    tile = lambda w_: pl.BlockSpec((t, w_), lambda i: (i, 0))
    full = lambda a: pl.BlockSpec(a.shape, lambda i: (0, 0), pipeline_mode=pl.Buffered(1))
    return pl.pallas_call(
        _tail_kernel,
        grid=(n // t,),
        in_specs=[tile(D_MODEL), tile(ATTN_WIDTH), tile(SSD_WIDTH), tile(SSD_WIDTH),
                  tile(PLE_DIM), full(gssd), full(gmix),
                  full(wout), full(gpre), full(wup), full(wdown), full(gmlp), full(wgate),
                  full(wproj), full(gple)],
        out_specs=tile(D_MODEL),
        out_shape=jax.ShapeDtypeStruct((n, D_MODEL), F32),
        compiler_params=pltpu.CompilerParams(
            dimension_semantics=("arbitrary",), vmem_limit_bytes=VMEM_LIMIT),
        name="tail",
    )(x2, attn2, y2, z2, p2, gssd, gmix, wout, gpre, wup, wdown, gmlp, wgate, wproj, gple)


def _pad_lanes(a):
    return jnp.pad(a, ((0, 0), (0, LANES - a.shape[-1])))


def kernel(x, p, positions, norm_mix_pre, norm_mix_post, w_in, conv_w, conv_b, dt_bias, a_log,
           d_skip, ssd_norm_g, w_out, norm_mlp_pre, norm_mlp_post, w_up, w_down, w_ple_gate,
           w_ple_proj, norm_ple_post):
    b, s, _ = x.shape
    half = HEAD_DIM // 2
    invf = (ROPE_THETA ** (-jnp.arange(half, dtype=F32) * 2.0 / HEAD_DIM)).reshape(half, 1)
    pos3 = positions.reshape(b, 1, s)
    row = lambda a: a.reshape(1, -1)

    h = x
    for i in range(w_in.shape[0]):
        *qkv, z, y = _in_proj(
            h, pos3, invf, row(norm_mix_pre[i]), w_in[i].astype(BF16),
            conv_w[i], row(conv_b[i]), _pad_lanes(row(dt_bias[i])), _pad_lanes(row(a_log[i])),
            jnp.pad(a_log[i], (0, SSD_HEADS)).reshape(2 * SSD_HEADS, 1),
            row(jnp.repeat(d_skip[i], SSD_HEAD_DIM)))
        attn = _attention(qkv)
        h = _tail(h.reshape(b * s, D_MODEL), attn.reshape(b * s, ATTN_WIDTH),
                  y.reshape(b * s, SSD_WIDTH), z.reshape(b * s, SSD_WIDTH),
                  p[i].reshape(b * s, PLE_DIM), row(ssd_norm_g[i]),
                  row(norm_mix_post[i]), w_out[i].astype(BF16), row(norm_mlp_pre[i]),
                  w_up[i].astype(BF16), w_down[i].astype(BF16), row(norm_mlp_post[i]),
                  w_ple_gate[i].astype(BF16), w_ple_proj[i].astype(BF16),
                  row(norm_ple_post[i])).reshape(b, s, D_MODEL)
    return h
```

```python
import functools
import math

import jax
import jax.numpy as jnp
from jax import lax
from jax.experimental import pallas as pl
from jax.experimental.pallas import tpu as pltpu

D_MODEL = 1024
PLE_DIM = 256
N_HEADS = 8
HEAD_DIM = 64
ATTN_WIDTH = N_HEADS * HEAD_DIM
SSD_HEADS = 8
SSD_HEAD_DIM = 64
SSD_WIDTH = SSD_HEADS * SSD_HEAD_DIM
SSD_STATE = 128
CONV_WIDTH = 4
CONV_CH = SSD_WIDTH + 2 * SSD_STATE
CHUNK = 128
D_FF = 4 * D_MODEL
ROPE_THETA = 10000.0
BRANCHES = ((128, 1), (512, 4), (2048, 16))
DILATIONS = tuple(d for _, d in BRANCHES)
ATTN_BLOCK = 128
NORM_EPS = 1e-6
Q_SCALE = HEAD_DIM ** -0.5 * math.log2(math.e)

LANES = 128
BF16_ROWS = 16
TAIL_WEIGHT_ROWS = (ATTN_WIDTH + SSD_WIDTH, D_MODEL, D_FF, D_MODEL, PLE_DIM)
CONV_HALO = 8
PROJ_TILE = 512
TAIL_TILE = 512
FF_CHUNK = 1024
MERGE_ROWS = 256
ATTN_UNROLL = 4
ATTN_GROUP = ATTN_BLOCK * max(DILATIONS)
VMEM_LIMIT = 56 * 1024 * 1024

F32 = jnp.float32
BF16 = jnp.bfloat16


def _dot(a, b):
    return jnp.dot(a, b, preferred_element_type=F32)


def _dot_nt(a, b):
    return lax.dot_general(a, b, (((1,), (1,)), ((), ())), preferred_element_type=F32)


def _sigmoid(x):
    return 1.0 / (1.0 + jnp.exp(-x))


def _rms(x):
    return x * lax.rsqrt(jnp.mean(x * x, axis=-1, keepdims=True) + NORM_EPS)


def _split3(a):
    hi = a.astype(BF16)
    rest = a - hi.astype(F32)
    mid = rest.astype(BF16)
    return hi, mid, (rest - mid.astype(F32)).astype(BF16)


def _ssd_prepare(bm, cm, dt, a_row, a_col, tril):
    cs = sum(_dot(tril, piece) for piece in _split3(dt * a_row))
    dt_t = dt.T[0:2 * SSD_HEADS]
    cs_t = sum(_dot_nt(piece, tril) for piece in _split3(dt_t * a_col))
    w_t = dt_t * jnp.exp2(cs_t[:, CHUNK - 1:CHUNK] - cs_t)
    return dict(cs=cs, cs_dt_t=cs_t - jnp.log2(dt_t), w_t=w_t, cb=_dot_nt(cm, bm),
                cm_f=cm.astype(F32), bm_t=bm.astype(F32).T)


def _ssd_chunk(xs_b, pre, dskip, prev, causal, low):
    zero = jnp.zeros((CHUNK, LANES), BF16)
    ys, nxt = [], []
    for g in range(SSD_WIDTH // LANES):
        x2 = xs_b[:, g * LANES:(g + 1) * LANES]
        rhs = jnp.concatenate([x2, prev[g].astype(BF16)], axis=0)
        y_lhs, inj_lhs, dec2 = [], [], []
        for h in (2 * g, 2 * g + 1):
            cs_h = jnp.broadcast_to(pre["cs"][:, h:h + 1], (CHUNK, CHUNK))
            grow = jnp.exp2(cs_h)
            seg = jnp.exp2(jnp.where(causal, cs_h - pre["cs_dt_t"][h:h + 1, :], -jnp.inf))
            y_lhs.append(jnp.concatenate([pre["cb"] * seg, pre["cm_f"] * grow],
                                         axis=1).astype(BF16))
            inj_lhs.append((pre["bm_t"] * pre["w_t"][h:h + 1, :]).astype(BF16))
            dec2.append(grow[CHUNK - 1:CHUNK, :])
        y_pair = _dot(jnp.concatenate(y_lhs, axis=0), rhs)
        ys.append(jnp.where(low, y_pair[:CHUNK], y_pair[CHUNK:]))
        x_split = jnp.concatenate([jnp.where(low, x2, zero), jnp.where(low, zero, x2)], axis=0)
        inject = _dot(jnp.concatenate(inj_lhs, axis=1), x_split)
        nxt.append(prev[g] * jnp.where(low[0:1], dec2[0], dec2[1]) + inject)

    return (jnp.concatenate(ys, axis=1) + dskip * xs_b.astype(F32)).astype(BF16), nxt


def _in_proj_kernel(x_ref, pos_ref, invf_ref, g_ref, win_ref, convw_ref, convb_ref, dtb_ref,
                    alog_ref, alogc_ref, dskip_ref, *rest):
    n_cast = len(TAIL_WEIGHT_ROWS)
    cast_src, rest = rest[:n_cast], rest[n_cast:]
    widths = (ATTN_WIDTH, ATTN_WIDTH, ATTN_WIDTH, SSD_WIDTH, CONV_CH, SSD_HEADS)
    edges = [sum(widths[:c]) for c in range(len(widths) + 1)]
    wq_ref, wk_ref, wv_ref, wz_ref, wxbc_ref, wdt_ref = (
        win_ref.at[:, lo:hi] for lo, hi in zip(edges, edges[1:]))
    n_qkv = 3 * len(DILATIONS)
    qkv_refs = rest[:n_qkv]
    z_ref, y_ref = rest[n_qkv:n_qkv + 2]
    cast_dst = rest[n_qkv + 2:n_qkv + 2 + n_cast]
    cbuf, qs, ks, vs = rest[n_qkv + 2 + n_cast:n_qkv + 6 + n_cast]
    stages, state = rest[n_qkv + 6 + n_cast:-1], rest[-1]
    t = x_ref.shape[1]

    @pl.when(pl.program_id(1) == 0)
    def _():
        cbuf[0:CONV_HALO, :] = jnp.zeros((CONV_HALO, CONV_CH), F32)
        state[...] = jnp.zeros(state.shape, F32)

    u = (_rms(x_ref[0]) * g_ref[...]).astype(BF16)

    ang = invf_ref[...] * pos_ref[0].astype(F32)
    cos4 = jnp.concatenate([jnp.cos(ang)] * 4, axis=0).T
    sin4 = jnp.concatenate([jnp.sin(ang)] * 4, axis=0).T
    lane = lax.broadcasted_iota(jnp.int32, (t, LANES), 1)
    first_half = (lane % HEAD_DIM) < (HEAD_DIM // 2)
    sin_signed = jnp.where(first_half, -sin4, sin4)

    def rope(v):
        partner = jnp.where(first_half, pltpu.roll(v, LANES - HEAD_DIM // 2, 1),
                            pltpu.roll(v, HEAD_DIM // 2, 1))
        return v * cos4 + partner * sin_signed

    def emit_layouts(ti):
        planes, d_prev = (qs, ks, vs), 1
        for bi, d in enumerate(DILATIONS):
            f = d // d_prev
            staged = stages[3 * (bi - 1):3 * bi] if 0 < bi < len(DILATIONS) - 1 else (None,) * 3
            src, stage, dst = planes[ti], staged[ti], qkv_refs[3 * bi + ti]
            for r in range(d):
                start = (r % d_prev) * (t // d_prev) + r // d_prev
                for g in range(ATTN_WIDTH // LANES):
                    val = src[g] if d == 1 else src[g, pl.ds(start, t // d, stride=f), :]
                    if stage is not None:
                        stage[g, r * (t // d):(r + 1) * (t // d), :] = val
                    c = r * ATTN_WIDTH + g * LANES
                    dst[0, :, c:c + LANES] = val.astype(BF16)
            if staged[0] is not None:
                planes, d_prev = staged, d

    lanes = [slice(g * LANES, (g + 1) * LANES) for g in range(ATTN_WIDTH // LANES)]
    xbc = _dot(u, wxbc_ref[...])
    dtr = jnp.pad(_dot(u, wdt_ref[...]), ((0, 0), (0, LANES - SSD_HEADS))) + dtb_ref[...]
    q = _dot(u, wq_ref[...])

    cbuf[CONV_HALO:CONV_HALO + t, :] = xbc
    w = convw_ref[...]
    y = convb_ref[...] + w[CONV_WIDTH - 1:CONV_WIDTH] * xbc
    for s in range(1, CONV_WIDTH):
        y = y + w[CONV_WIDTH - 1 - s:CONV_WIDTH - s] * cbuf[CONV_HALO - s:CONV_HALO - s + t, :]
    cbuf[0:CONV_HALO, :] = cbuf[t:t + CONV_HALO, :]
    act = y * _sigmoid(y)
    xs_b = act[:, :SSD_WIDTH].astype(BF16)
    bm = act[:, SSD_WIDTH:SSD_WIDTH + SSD_STATE].astype(BF16)
    cm = act[:, SSD_WIDTH + SSD_STATE:].astype(BF16)

    dt = jnp.maximum(dtr, 0.0) + jnp.log1p(jnp.exp(-jnp.abs(dtr)))

    k = _dot(u, wk_ref[...])
    for g, sl in enumerate(lanes):
        qs[g] = rope(q[:, sl]) * Q_SCALE
    v = _dot(u, wv_ref[...])
    for g, sl in enumerate(lanes):
        ks[g] = rope(k[:, sl])

    row = lax.broadcasted_iota(jnp.int32, (CHUNK, CHUNK), 0)
    col = lax.broadcasted_iota(jnp.int32, (CHUNK, CHUNK), 1)
    causal = row >= col
    low = col < SSD_HEAD_DIM
    tril = causal.astype(BF16)
    a_row = -jnp.exp(alog_ref[...]) * math.log2(math.e)
    a_col = -jnp.exp(alogc_ref[...]) * math.log2(math.e)
    groups = SSD_WIDTH // LANES
    st = [state[:, g * LANES:(g + 1) * LANES] for g in range(groups)]

    chunk_rows = [slice(c * CHUNK, (c + 1) * CHUNK) for c in range(t // CHUNK)]
    pres = [_ssd_prepare(bm[rows], cm[rows], dt[rows], a_row, a_col, tril) for rows in chunk_rows]

    def ssd_step(c, st):
        rows = chunk_rows[c]
        yc, st = _ssd_chunk(xs_b[rows], pres[c], dskip_ref[...], st, causal, low)
        y_ref[0, rows, :] = yc
        return st

    assert t // CHUNK == 4
    st = ssd_step(0, st)
    z_ref[0] = _dot(u, wz_ref[...]).astype(BF16)
    emit_layouts(0)
    st = ssd_step(1, st)
    emit_layouts(1)
    st = ssd_step(2, st)
    for g, sl in enumerate(lanes):
        vs[g] = v[:, sl]
    emit_layouts(2)
    st = ssd_step(3, st)
    for g in range(groups):
        state[:, g * LANES:(g + 1) * LANES] = st[g]

    for src, dst in zip(cast_src, cast_dst):
        dst[...] = src[...].astype(BF16)


def _in_proj(x, pos3, invf, g, win, convw, convb, dtb, alog, alogc, dskip, tail_weights):
    b, s, _ = x.shape
    t = PROJ_TILE
    tile = lambda w_: pl.BlockSpec((1, t, w_), lambda bi, j: (bi, j, 0))
    full = lambda a: pl.BlockSpec(a.shape, lambda bi, j: (0,) * a.ndim)
    wide = jax.ShapeDtypeStruct((b, s, ATTN_WIDTH), BF16)
    qkv_specs, qkv_shapes = [], []
    for d in DILATIONS:
        qkv_specs += [pl.BlockSpec((1, t // d, d * ATTN_WIDTH), lambda bi, j: (bi, j, 0))] * 3
        qkv_shapes += [jax.ShapeDtypeStruct((b, s // d, d * ATTN_WIDTH), BF16)] * 3
    plane = pltpu.VMEM((ATTN_WIDTH // LANES, t, LANES), F32)
    steps = b * (s // t)
    cast_specs = []
    for w_ in tail_weights:
        assert w_.shape[0] in TAIL_WEIGHT_ROWS and w_.shape[0] % steps == 0
        per_step = w_.shape[0] // steps
        rows = max(per_step, BF16_ROWS)
        cast_specs.append(pl.BlockSpec(
            (rows, w_.shape[1]),
            lambda bi, j, per_step=per_step, rows=rows: ((bi * (s // t) + j) * per_step // rows, 0)))
    return pl.pallas_call(
        _in_proj_kernel,
        grid=(b, s // t),
        in_specs=[tile(D_MODEL), pl.BlockSpec((1, 1, t), lambda bi, j: (bi, 0, j)),
                  full(invf), full(g), full(win), full(convw), full(convb), full(dtb), full(alog), full(alogc),
                  full(dskip)] + cast_specs,
        out_specs=qkv_specs + [tile(ATTN_WIDTH), tile(SSD_WIDTH)] + cast_specs,
        out_shape=qkv_shapes + [wide, wide]
                  + [jax.ShapeDtypeStruct(w_.shape, BF16) for w_ in tail_weights],
        scratch_shapes=[pltpu.VMEM((t + CONV_HALO, CONV_CH), F32)]
                       + [plane] * (3 * (len(DILATIONS) - 1))
                       + [pltpu.VMEM((SSD_STATE, SSD_WIDTH), F32)],
        compiler_params=pltpu.CompilerParams(
            dimension_semantics=("arbitrary", "arbitrary"), vmem_limit_bytes=VMEM_LIMIT),
        name="in_proj",
    )(x, pos3, invf, g, win, convw, convb, dtb, alog, alogc, dskip, *tail_weights)


def _attn_block(has_prev, span, q, kp, kc, vp, vc):
    blk = ATTN_BLOCK
    qi = lax.broadcasted_iota(jnp.int32, (blk, 2 * blk), 0)
    ki = lax.broadcasted_iota(jnp.int32, (blk, 2 * blk), 1)
    dist = qi + blk - ki
    valid = (dist >= 0) & (dist <= span) & ((ki >= blk) | has_prev)
    bias = jnp.where(valid, 0.0, -jnp.inf).astype(F32)
    bias = jnp.concatenate([bias, bias], axis=0)
    low = lax.broadcasted_iota(jnp.int32, (blk, LANES), 1) < HEAD_DIM

    k = jnp.concatenate([kp, kc], axis=0)
    v = jnp.concatenate([vp, vc], axis=0)
    outs, maxes, sums = [], [], []
    for g in range(ATTN_WIDTH // LANES):
        sl = slice(g * LANES, (g + 1) * LANES)
        q2, k2, v2 = q[:, sl], k[:, sl], v[:, sl]
        zero = jnp.zeros_like(q2)
        qq = jnp.concatenate([jnp.where(low, q2, zero), jnp.where(low, zero, q2)], axis=0)
        s = _dot_nt(qq, k2) + bias
        m = jnp.max(s, axis=-1, keepdims=True)
        p = jnp.exp2(s - m)
        l = jnp.sum(p, axis=-1, keepdims=True)
        o = _dot(p.astype(BF16), v2)
        outs.append(jnp.where(low, o[:blk], o[blk:]))
        maxes.append(jnp.where(low, m[:blk], m[blk:]))
        sums.append(jnp.where(low, l[:blk], l[blk:]))
    return outs, maxes, sums


def _attn_kernel(*refs):
    nbr = len(BRANCHES)
    in_refs = refs[:5 * nbr]
    out_ref = refs[5 * nbr]
    acc_o = refs[5 * nbr + 1:5 * nbr + 1 + nbr]
    acc_m = refs[5 * nbr + 1 + nbr:5 * nbr + 1 + 2 * nbr]
    acc_l = refs[5 * nbr + 1 + 2 * nbr:]
    group_steps = ATTN_GROUP // ATTN_BLOCK
    blk, wid = ATTN_BLOCK, ATTN_WIDTH

    for sub in range(ATTN_UNROLL):
        i = pl.program_id(1) * ATTN_UNROLL + sub
        local = i % group_steps
        for bi, (window, d) in enumerate(BRANCHES):
            q_ref, kp_ref, kc_ref, vp_ref, vc_ref = in_refs[5 * bi:5 * bi + 5]
            if d == 1:
                rows = slice(sub * blk, (sub + 1) * blk)
                before = slice((sub - 1) * blk, sub * blk)
                ops = (q_ref[0, rows, :],
                       kp_ref[0] if sub == 0 else kc_ref[0, before, :], kc_ref[0, rows, :],
                       vp_ref[0] if sub == 0 else vc_ref[0, before, :], vc_ref[0, rows, :])
                has_prev = (i > 0) if sub == 0 else True
            else:
                cols = slice(sub * wid, (sub + 1) * wid)
                ops = tuple(ref[0, :, cols] for ref in (q_ref, kp_ref, kc_ref, vp_ref, vc_ref))
                has_prev = (i // d) > 0
            o, m, l = _attn_block(has_prev, window // d, *ops)
            start = (local // d) * (d * blk) + i % d
            rows = pl.ds(start, blk) if d == 1 else pl.ds(start, blk, stride=d)
            for g in range(wid // LANES):
                acc_o[bi][g, rows, :] = o[g]
                acc_m[bi][g, rows, :] = m[g]
                acc_l[bi][g, rows, :] = l[g]

    @pl.when(local == group_steps - 1)
    def _():
        total = lambda xs: functools.reduce(lambda a, b_: a + b_, xs)

        def merge(c, carry):
            rows = pl.ds(pl.multiple_of(c * MERGE_ROWS, MERGE_ROWS), MERGE_ROWS)
            for g in range(ATTN_WIDTH // LANES):
                ms = [a[g, rows, :] for a in acc_m]
                top = functools.reduce(jnp.maximum, ms)
                es = [jnp.exp2(m - top) for m in ms]
                den = total([e * a[g, rows, :] for e, a in zip(es, acc_l)])
                num = total([e * a[g, rows, :] for e, a in zip(es, acc_o)])
                out_ref[0, rows, g * LANES:(g + 1) * LANES] = (num / den).astype(BF16)
            return carry

        lax.fori_loop(0, ATTN_GROUP // MERGE_ROWS, merge, 0)


def _attention(qkv):
    b = qkv[0].shape[0]
    s = qkv[0].shape[1] * DILATIONS[0]
    un = ATTN_UNROLL
    steps = s // (ATTN_BLOCK * un)
    group_steps = ATTN_GROUP // (ATTN_BLOCK * un)
    in_specs = []
    for d in DILATIONS:
        if d == 1:
            cur = pl.BlockSpec((1, un * ATTN_BLOCK, ATTN_WIDTH), lambda bi, j: (bi, j, 0))
            prev = pl.BlockSpec((1, ATTN_BLOCK, ATTN_WIDTH),
                                lambda bi, j: (bi, jnp.maximum(j * un - 1, 0), 0))
        else:
            assert d % un == 0
            cur = pl.BlockSpec((1, ATTN_BLOCK, un * ATTN_WIDTH),
                               lambda bi, j, d=d: (bi, (j * un) // d, (j * un % d) // un))
            prev = pl.BlockSpec((1, ATTN_BLOCK, un * ATTN_WIDTH),
                                lambda bi, j, d=d: (bi, jnp.maximum((j * un) // d - 1, 0),
                                                    (j * un % d) // un))
        in_specs += [cur, prev, cur, prev, cur]
    args = []
    for bi in range(len(DILATIONS)):
        q, k, v = qkv[3 * bi:3 * bi + 3]
        args += [q, k, k, v, v]
    nbr = len(BRANCHES)
    return pl.pallas_call(
        _attn_kernel,
        grid=(b, steps),
        in_specs=in_specs,
        out_specs=pl.BlockSpec((1, ATTN_GROUP, ATTN_WIDTH), lambda bi, i: (bi, i // group_steps, 0)),
        out_shape=jax.ShapeDtypeStruct((b, s, ATTN_WIDTH), BF16),
        scratch_shapes=[pltpu.VMEM((ATTN_WIDTH // LANES, ATTN_GROUP, LANES), F32)] * (3 * nbr),
        compiler_params=pltpu.CompilerParams(
            dimension_semantics=("arbitrary", "arbitrary"), vmem_limit_bytes=VMEM_LIMIT),
        name="dilated_attention",
    )(*args)


def _tail_kernel(x_ref, attn_ref, y_ref, z_ref, p_ref, gssd_ref, gmix_ref, wout_ref, gpre_ref,
                 wup_ref, wdown_ref, gmlp_ref, wgate_ref, wproj_ref, gple_ref, out_ref):
    def gate_ssd(d):
        zf = z_ref[d["rows"], :].astype(F32)
        y = y_ref[d["rows"], :].astype(F32) * (zf * _sigmoid(zf))
        d["y"] = (_rms(y) * gssd_ref[...]).astype(BF16)

    def out_proj(d):
        d["mix"] = (_dot(attn_ref[d["rows"], :], wout_ref[0:ATTN_WIDTH, :])
                    + _dot(d["y"], wout_ref[ATTN_WIDTH:, :]))

    def norm_mix(d):
        d["h"] = x_ref[d["rows"], :] + _rms(d["mix"]) * gmix_ref[...]
        d["u"] = (_rms(d["h"]) * gpre_ref[...]).astype(BF16)

    def mlp(d):
        ff = jnp.zeros(d["h"].shape, F32)
        for c in range(D_FF // FF_CHUNK):
            sl = slice(c * FF_CHUNK, (c + 1) * FF_CHUNK)
            hid = jnp.square(jnp.maximum(_dot(d["u"], wup_ref[:, sl]), 0.0)).astype(BF16)
            ff = ff + _dot(hid, wdown_ref[sl, :])
        d["ff"] = ff

    def norm_mlp(d):
        d["h"] = d["h"] + _rms(d["ff"]) * gmlp_ref[...]

    def ple_proj(d):
        d["gate"] = _dot(d["h"].astype(BF16), wgate_ref[...])
        d["ple"] = _dot(p_ref[d["rows"], :].astype(BF16), wproj_ref[...])

    def norm_ple(d):
        ple = d["ple"] * _sigmoid(d["gate"])
        out_ref[d["rows"], :] = d["h"] + _rms(ple) * gple_ref[...]

    half = x_ref.shape[0] // 2
    a = dict(rows=slice(0, half))
    b = dict(rows=slice(half, 2 * half))
    for stage, d in ((gate_ssd, a), (out_proj, a), (gate_ssd, b), (norm_mix, a), (out_proj, b),
                     (mlp, a), (norm_mix, b), (mlp, b), (norm_mlp, a), (ple_proj, a),
                     (norm_mlp, b), (norm_ple, a), (ple_proj, b), (norm_ple, b)):
        stage(d)


def _tail(x2, attn2, y2, z2, p2, gssd, gmix, wout, gpre, wup, wdown, gmlp, wgate, wproj, gple):
    n = x2.shape[0]
    t = TAIL_TILE
    tile = lambda w_: pl.BlockSpec((t, w_), lambda i: (i, 0))
    full = lambda a: pl.BlockSpec(a.shape, lambda i: (0, 0), pipeline_mode=pl.Buffered(1))
    return pl.pallas_call(
        _tail_kernel,
        grid=(n // t,),
        in_specs=[tile(D_MODEL), tile(ATTN_WIDTH), tile(SSD_WIDTH), tile(SSD_WIDTH),
                  tile(PLE_DIM), full(gssd), full(gmix),
                  full(wout), full(gpre), full(wup), full(wdown), full(gmlp), full(wgate),
                  full(wproj), full(gple)],
        out_specs=tile(D_MODEL),
        out_shape=jax.ShapeDtypeStruct((n, D_MODEL), F32),
        compiler_params=pltpu.CompilerParams(
            dimension_semantics=("arbitrary",), vmem_limit_bytes=VMEM_LIMIT),
        name="tail",
    )(x2, attn2, y2, z2, p2, gssd, gmix, wout, gpre, wup, wdown, gmlp, wgate, wproj, gple)


def _pad_lanes(a):
    return jnp.pad(a, ((0, 0), (0, LANES - a.shape[-1])))


def kernel(x, p, positions, norm_mix_pre, norm_mix_post, w_in, conv_w, conv_b, dt_bias, a_log,
           d_skip, ssd_norm_g, w_out, norm_mlp_pre, norm_mlp_post, w_up, w_down, w_ple_gate,
           w_ple_proj, norm_ple_post):
    b, s, _ = x.shape
    half = HEAD_DIM // 2
    invf = (ROPE_THETA ** (-jnp.arange(half, dtype=F32) * 2.0 / HEAD_DIM)).reshape(half, 1)
    pos3 = positions.reshape(b, 1, s)
    row = lambda a: a.reshape(1, -1)

    h = x
    for i in range(w_in.shape[0]):
        *qkv, z, y, wout, wup, wdown, wgate, wproj = _in_proj(
            h, pos3, invf, row(norm_mix_pre[i]), w_in[i].astype(BF16),
            conv_w[i], row(conv_b[i]), _pad_lanes(row(dt_bias[i])), _pad_lanes(row(a_log[i])),
            jnp.pad(a_log[i], (0, SSD_HEADS)).reshape(2 * SSD_HEADS, 1),
            row(jnp.repeat(d_skip[i], SSD_HEAD_DIM)),
            [w_out[i], w_up[i], w_down[i], w_ple_gate[i], w_ple_proj[i]])
        attn = _attention(qkv)
        h = _tail(h.reshape(b * s, D_MODEL), attn.reshape(b * s, ATTN_WIDTH),
                  y.reshape(b * s, SSD_WIDTH), z.reshape(b * s, SSD_WIDTH),
                  p[i].reshape(b * s, PLE_DIM), row(ssd_norm_g[i]),
                  row(norm_mix_post[i]), wout, row(norm_mlp_pre[i]), wup, wdown,
                  row(norm_mlp_post[i]), wgate, wproj,
                  row(norm_ple_post[i])).reshape(b, s, D_MODEL)
    return h
```

```python
import functools
import math

import jax
import jax.numpy as jnp
from jax import lax
from jax.experimental import pallas as pl
from jax.experimental.pallas import tpu as pltpu

D_MODEL = 1024
PLE_DIM = 256
N_HEADS = 8
HEAD_DIM = 64
ATTN_WIDTH = N_HEADS * HEAD_DIM
SSD_HEADS = 8
SSD_HEAD_DIM = 64
SSD_WIDTH = SSD_HEADS * SSD_HEAD_DIM
SSD_STATE = 128
CONV_WIDTH = 4
CONV_CH = SSD_WIDTH + 2 * SSD_STATE
CHUNK = 128
D_FF = 4 * D_MODEL
ROPE_THETA = 10000.0
BRANCHES = ((128, 1), (512, 4), (2048, 16))
DILATIONS = tuple(d for _, d in BRANCHES)
ATTN_BLOCK = 128
NORM_EPS = 1e-6
Q_SCALE = HEAD_DIM ** -0.5 * math.log2(math.e)

LANES = 128
BF16_ROWS = 16
TAIL_WEIGHT_ROWS = (ATTN_WIDTH + SSD_WIDTH, D_MODEL, D_FF, D_MODEL, PLE_DIM)
CONV_HALO = 8
PROJ_TILE = 512
TAIL_TILE = 512
FF_CHUNK = 1024
MERGE_ROWS = 256
ATTN_UNROLL = 4
ATTN_GROUP = ATTN_BLOCK * max(DILATIONS)
VMEM_LIMIT = 56 * 1024 * 1024

F32 = jnp.float32
BF16 = jnp.bfloat16


def _dot(a, b):
    return jnp.dot(a, b, preferred_element_type=F32)


def _dot_nt(a, b):
    return lax.dot_general(a, b, (((1,), (1,)), ((), ())), preferred_element_type=F32)


def _sigmoid(x):
    return 1.0 / (1.0 + jnp.exp(-x))


def _rms(x):
    return x * lax.rsqrt(jnp.mean(x * x, axis=-1, keepdims=True) + NORM_EPS)


def _split3(a):
    hi = a.astype(BF16)
    rest = a - hi.astype(F32)
    mid = rest.astype(BF16)
    return hi, mid, (rest - mid.astype(F32)).astype(BF16)


def _ssd_prepare(bm, cm, dt, a_row, a_col, tril):
    cs = sum(_dot(tril, piece) for piece in _split3(dt * a_row))
    dt_t = dt.T[0:2 * SSD_HEADS]
    cs_t = sum(_dot_nt(piece, tril) for piece in _split3(dt_t * a_col))
    w_t = dt_t * jnp.exp2(cs_t[:, CHUNK - 1:CHUNK] - cs_t)
    return dict(cs=cs, cs_dt_t=cs_t - jnp.log2(dt_t), w_t=w_t, cb=_dot_nt(cm, bm),
                cm_f=cm.astype(F32), bm_t=bm.astype(F32).T)


def _ssd_chunk(xs_b, pre, dskip, prev, causal, low):
    zero = jnp.zeros((CHUNK, LANES), BF16)
    ys, nxt = [], []
    for g in range(SSD_WIDTH // LANES):
        x2 = xs_b[:, g * LANES:(g + 1) * LANES]
        rhs = jnp.concatenate([x2, prev[g].astype(BF16)], axis=0)
        y_lhs, inj_lhs, dec2 = [], [], []
        for h in (2 * g, 2 * g + 1):
            cs_h = jnp.broadcast_to(pre["cs"][:, h:h + 1], (CHUNK, CHUNK))
            grow = jnp.exp2(cs_h)
            seg = jnp.exp2(jnp.where(causal, cs_h - pre["cs_dt_t"][h:h + 1, :], -jnp.inf))
            y_lhs.append(jnp.concatenate([pre["cb"] * seg, pre["cm_f"] * grow],
                                         axis=1).astype(BF16))
            inj_lhs.append((pre["bm_t"] * pre["w_t"][h:h + 1, :]).astype(BF16))
            dec2.append(grow[CHUNK - 1:CHUNK, :])
        y_pair = _dot(jnp.concatenate(y_lhs, axis=0), rhs)
        ys.append(jnp.where(low, y_pair[:CHUNK], y_pair[CHUNK:]))
        x_split = jnp.concatenate([jnp.where(low, x2, zero), jnp.where(low, zero, x2)], axis=0)
        inject = _dot(jnp.concatenate(inj_lhs, axis=1), x_split)
        nxt.append(prev[g] * jnp.where(low[0:1], dec2[0], dec2[1]) + inject)

    return (jnp.concatenate(ys, axis=1) + dskip * xs_b.astype(F32)).astype(BF16), nxt


def _in_proj_kernel(x_ref, pos_ref, invf_ref, g_ref, win_ref, convw_ref, convb_ref, dtb_ref,
                    alog_ref, alogc_ref, dskip_ref, *rest):
    n_cast = len(TAIL_WEIGHT_ROWS)
    cast_src, rest = rest[:n_cast], rest[n_cast:]
    n_qkv = 3 * len(DILATIONS)
    qkv_refs = rest[:n_qkv]
    z_ref, y_ref = rest[n_qkv:n_qkv + 2]
    cast_dst = rest[n_qkv + 2:n_qkv + 2 + n_cast]
    cbuf, qs, ks, vs = rest[n_qkv + 2 + n_cast:n_qkv + 6 + n_cast]
    stages, state, w_s = rest[n_qkv + 6 + n_cast:-2], rest[-2], rest[-1]
    widths = (ATTN_WIDTH, ATTN_WIDTH, ATTN_WIDTH, SSD_WIDTH, CONV_CH, SSD_HEADS)
    edges = [sum(widths[:c]) for c in range(len(widths) + 1)]
    wq_ref, wk_ref, wv_ref, wz_ref, wxbc_ref, wdt_ref = (
        w_s.at[:, lo:hi] for lo, hi in zip(edges, edges[1:]))
    t = x_ref.shape[1]

    @pl.when((pl.program_id(0) == 0) & (pl.program_id(1) == 0))
    def _():
        w_s[...] = win_ref[...].astype(BF16)

    @pl.when(pl.program_id(1) == 0)
    def _():
        cbuf[0:CONV_HALO, :] = jnp.zeros((CONV_HALO, CONV_CH), F32)
        state[...] = jnp.zeros(state.shape, F32)

    u = (_rms(x_ref[0]) * g_ref[...]).astype(BF16)

    ang = invf_ref[...] * pos_ref[0].astype(F32)
    cos4 = jnp.concatenate([jnp.cos(ang)] * 4, axis=0).T
    sin4 = jnp.concatenate([jnp.sin(ang)] * 4, axis=0).T
    lane = lax.broadcasted_iota(jnp.int32, (t, LANES), 1)
    first_half = (lane % HEAD_DIM) < (HEAD_DIM // 2)
    sin_signed = jnp.where(first_half, -sin4, sin4)

    def rope(v):
        partner = jnp.where(first_half, pltpu.roll(v, LANES - HEAD_DIM // 2, 1),
                            pltpu.roll(v, HEAD_DIM // 2, 1))
        return v * cos4 + partner * sin_signed

    def emit_layouts(ti):
        planes, d_prev = (qs, ks, vs), 1
        for bi, d in enumerate(DILATIONS):
            f = d // d_prev
            staged = stages[3 * (bi - 1):3 * bi] if 0 < bi < len(DILATIONS) - 1 else (None,) * 3
            src, stage, dst = planes[ti], staged[ti], qkv_refs[3 * bi + ti]
            for r in range(d):
                start = (r % d_prev) * (t // d_prev) + r // d_prev
                for g in range(ATTN_WIDTH // LANES):
                    val = src[g] if d == 1 else src[g, pl.ds(start, t // d, stride=f), :]
                    if stage is not None:
                        stage[g, r * (t // d):(r + 1) * (t // d), :] = val
                    c = r * ATTN_WIDTH + g * LANES
                    dst[0, :, c:c + LANES] = val.astype(BF16)
            if staged[0] is not None:
                planes, d_prev = staged, d

    lanes = [slice(g * LANES, (g + 1) * LANES) for g in range(ATTN_WIDTH // LANES)]
    xbc = _dot(u, wxbc_ref[...])
    dtr = jnp.pad(_dot(u, wdt_ref[...]), ((0, 0), (0, LANES - SSD_HEADS))) + dtb_ref[...]
    q = _dot(u, wq_ref[...])

    cbuf[CONV_HALO:CONV_HALO + t, :] = xbc
    w = convw_ref[...]
    y = convb_ref[...] + w[CONV_WIDTH - 1:CONV_WIDTH] * xbc
    for s in range(1, CONV_WIDTH):
        y = y + w[CONV_WIDTH - 1 - s:CONV_WIDTH - s] * cbuf[CONV_HALO - s:CONV_HALO - s + t, :]
    cbuf[0:CONV_HALO, :] = cbuf[t:t + CONV_HALO, :]
    act = y * _sigmoid(y)
    xs_b = act[:, :SSD_WIDTH].astype(BF16)
    bm = act[:, SSD_WIDTH:SSD_WIDTH + SSD_STATE].astype(BF16)
    cm = act[:, SSD_WIDTH + SSD_STATE:].astype(BF16)

    dt = jnp.maximum(dtr, 0.0) + jnp.log1p(jnp.exp(-jnp.abs(dtr)))

    k = _dot(u, wk_ref[...])
    for g, sl in enumerate(lanes):
        qs[g] = rope(q[:, sl]) * Q_SCALE
    v = _dot(u, wv_ref[...])
    for g, sl in enumerate(lanes):
        ks[g] = rope(k[:, sl])

    row = lax.broadcasted_iota(jnp.int32, (CHUNK, CHUNK), 0)
    col = lax.broadcasted_iota(jnp.int32, (CHUNK, CHUNK), 1)
    causal = row >= col
    low = col < SSD_HEAD_DIM
    tril = causal.astype(BF16)
    a_row = -jnp.exp(alog_ref[...]) * math.log2(math.e)
    a_col = -jnp.exp(alogc_ref[...]) * math.log2(math.e)
    groups = SSD_WIDTH // LANES
    st = [state[:, g * LANES:(g + 1) * LANES] for g in range(groups)]

    chunk_rows = [slice(c * CHUNK, (c + 1) * CHUNK) for c in range(t // CHUNK)]
    pres = [_ssd_prepare(bm[rows], cm[rows], dt[rows], a_row, a_col, tril) for rows in chunk_rows]

    def ssd_step(c, st):
        rows = chunk_rows[c]
        yc, st = _ssd_chunk(xs_b[rows], pres[c], dskip_ref[...], st, causal, low)
        y_ref[0, rows, :] = yc
        return st

    assert t // CHUNK == 4
    st = ssd_step(0, st)
    z_ref[0] = _dot(u, wz_ref[...]).astype(BF16)
    emit_layouts(0)
    st = ssd_step(1, st)
    emit_layouts(1)
    st = ssd_step(2, st)
    for g, sl in enumerate(lanes):
        vs[g] = v[:, sl]
    emit_layouts(2)
    st = ssd_step(3, st)
    for g in range(groups):
        state[:, g * LANES:(g + 1) * LANES] = st[g]

    for src, dst in zip(cast_src, cast_dst):
        dst[...] = src[...].astype(BF16)


def _in_proj(x, pos3, invf, g, win, convw, convb, dtb, alog, alogc, dskip, tail_weights):
    b, s, _ = x.shape
    t = PROJ_TILE
    tile = lambda w_: pl.BlockSpec((1, t, w_), lambda bi, j: (bi, j, 0))
    full = lambda a: pl.BlockSpec(a.shape, lambda bi, j: (0,) * a.ndim)
    wide = jax.ShapeDtypeStruct((b, s, ATTN_WIDTH), BF16)
    qkv_specs, qkv_shapes = [], []
    for d in DILATIONS:
        qkv_specs += [pl.BlockSpec((1, t // d, d * ATTN_WIDTH), lambda bi, j: (bi, j, 0))] * 3
        qkv_shapes += [jax.ShapeDtypeStruct((b, s // d, d * ATTN_WIDTH), BF16)] * 3
    plane = pltpu.VMEM((ATTN_WIDTH // LANES, t, LANES), F32)
    steps = b * (s // t)
    cast_specs = []
    for w_ in tail_weights:
        assert w_.shape[0] in TAIL_WEIGHT_ROWS and w_.shape[0] % steps == 0
        per_step = w_.shape[0] // steps
        rows = max(per_step, BF16_ROWS)
        cast_specs.append(pl.BlockSpec(
            (rows, w_.shape[1]),
            lambda bi, j, per_step=per_step, rows=rows: ((bi * (s // t) + j) * per_step // rows, 0)))
    return pl.pallas_call(
        _in_proj_kernel,
        grid=(b, s // t),
        in_specs=[tile(D_MODEL), pl.BlockSpec((1, 1, t), lambda bi, j: (bi, 0, j)),
                  full(invf), full(g),
                  pl.BlockSpec(win.shape, lambda bi, j: (0, 0), pipeline_mode=pl.Buffered(1)),
                  full(convw), full(convb), full(dtb), full(alog), full(alogc),
                  full(dskip)] + cast_specs,
        out_specs=qkv_specs + [tile(ATTN_WIDTH), tile(SSD_WIDTH)] + cast_specs,
        out_shape=qkv_shapes + [wide, wide]
                  + [jax.ShapeDtypeStruct(w_.shape, BF16) for w_ in tail_weights],
        scratch_shapes=[pltpu.VMEM((t + CONV_HALO, CONV_CH), F32)]
                       + [plane] * (3 * (len(DILATIONS) - 1))
                       + [pltpu.VMEM((SSD_STATE, SSD_WIDTH), F32), pltpu.VMEM(win.shape, BF16)],
        compiler_params=pltpu.CompilerParams(
            dimension_semantics=("arbitrary", "arbitrary"), vmem_limit_bytes=VMEM_LIMIT),
        name="in_proj",
    )(x, pos3, invf, g, win, convw, convb, dtb, alog, alogc, dskip, *tail_weights)


def _attn_block(has_prev, span, q, kp, kc, vp, vc):
    blk = ATTN_BLOCK
    qi = lax.broadcasted_iota(jnp.int32, (blk, 2 * blk), 0)
    ki = lax.broadcasted_iota(jnp.int32, (blk, 2 * blk), 1)
    dist = qi + blk - ki
    valid = (dist >= 0) & (dist <= span) & ((ki >= blk) | has_prev)
    bias = jnp.where(valid, 0.0, -jnp.inf).astype(F32)
    bias = jnp.concatenate([bias, bias], axis=0)
    low = lax.broadcasted_iota(jnp.int32, (blk, LANES), 1) < HEAD_DIM

    k = jnp.concatenate([kp, kc], axis=0)
    v = jnp.concatenate([vp, vc], axis=0)
    outs, maxes, sums = [], [], []
    for g in range(ATTN_WIDTH // LANES):
        sl = slice(g * LANES, (g + 1) * LANES)
        q2, k2, v2 = q[:, sl], k[:, sl], v[:, sl]
        zero = jnp.zeros_like(q2)
        qq = jnp.concatenate([jnp.where(low, q2, zero), jnp.where(low, zero, q2)], axis=0)
        s = _dot_nt(qq, k2) + bias
        m = jnp.max(s, axis=-1, keepdims=True)
        p = jnp.exp2(s - m)
        l = jnp.sum(p, axis=-1, keepdims=True)
        o = _dot(p.astype(BF16), v2)
        outs.append(jnp.where(low, o[:blk], o[blk:]))
        maxes.append(jnp.where(low, m[:blk], m[blk:]))
        sums.append(jnp.where(low, l[:blk], l[blk:]))
    return outs, maxes, sums


def _attn_kernel(*refs):
    nbr = len(BRANCHES)
    in_refs = refs[:5 * nbr]
    out_ref = refs[5 * nbr]
    acc_o = refs[5 * nbr + 1:5 * nbr + 1 + nbr]
    acc_m = refs[5 * nbr + 1 + nbr:5 * nbr + 1 + 2 * nbr]
    acc_l = refs[5 * nbr + 1 + 2 * nbr:]
    group_steps = ATTN_GROUP // ATTN_BLOCK
    blk, wid = ATTN_BLOCK, ATTN_WIDTH

    for sub in range(ATTN_UNROLL):
        i = pl.program_id(1) * ATTN_UNROLL + sub
        local = i % group_steps
        for bi, (window, d) in enumerate(BRANCHES):
            q_ref, kp_ref, kc_ref, vp_ref, vc_ref = in_refs[5 * bi:5 * bi + 5]
            if d == 1:
                rows = slice(sub * blk, (sub + 1) * blk)
                before = slice((sub - 1) * blk, sub * blk)
                ops = (q_ref[0, rows, :],
                       kp_ref[0] if sub == 0 else kc_ref[0, before, :], kc_ref[0, rows, :],
                       vp_ref[0] if sub == 0 else vc_ref[0, before, :], vc_ref[0, rows, :])
                has_prev = (i > 0) if sub == 0 else True
            else:
                cols = slice(sub * wid, (sub + 1) * wid)
                ops = tuple(ref[0, :, cols] for ref in (q_ref, kp_ref, kc_ref, vp_ref, vc_ref))
                has_prev = (i // d) > 0
            o, m, l = _attn_block(has_prev, window // d, *ops)
            start = (local // d) * (d * blk) + i % d
            rows = pl.ds(start, blk) if d == 1 else pl.ds(start, blk, stride=d)
            for g in range(wid // LANES):
                acc_o[bi][g, rows, :] = o[g]
                acc_m[bi][g, rows, :] = m[g]
                acc_l[bi][g, rows, :] = l[g]

    @pl.when(local == group_steps - 1)
    def _():
        total = lambda xs: functools.reduce(lambda a, b_: a + b_, xs)

        def merge(c, carry):
            rows = pl.ds(pl.multiple_of(c * MERGE_ROWS, MERGE_ROWS), MERGE_ROWS)
            for g in range(ATTN_WIDTH // LANES):
                ms = [a[g, rows, :] for a in acc_m]
                top = functools.reduce(jnp.maximum, ms)
                es = [jnp.exp2(m - top) for m in ms]
                den = total([e * a[g, rows, :] for e, a in zip(es, acc_l)])
                num = total([e * a[g, rows, :] for e, a in zip(es, acc_o)])
                out_ref[0, rows, g * LANES:(g + 1) * LANES] = (num / den).astype(BF16)
            return carry

        lax.fori_loop(0, ATTN_GROUP // MERGE_ROWS, merge, 0)


def _attention(qkv):
    b = qkv[0].shape[0]
    s = qkv[0].shape[1] * DILATIONS[0]
    un = ATTN_UNROLL
    steps = s // (ATTN_BLOCK * un)
    group_steps = ATTN_GROUP // (ATTN_BLOCK * un)
    in_specs = []
    for d in DILATIONS:
        if d == 1:
            cur = pl.BlockSpec((1, un * ATTN_BLOCK, ATTN_WIDTH), lambda bi, j: (bi, j, 0))
            prev = pl.BlockSpec((1, ATTN_BLOCK, ATTN_WIDTH),
                                lambda bi, j: (bi, jnp.maximum(j * un - 1, 0), 0))
        else:
            assert d % un == 0
            cur = pl.BlockSpec((1, ATTN_BLOCK, un * ATTN_WIDTH),
                               lambda bi, j, d=d: (bi, (j * un) // d, (j * un % d) // un))
            prev = pl.BlockSpec((1, ATTN_BLOCK, un * ATTN_WIDTH),
                                lambda bi, j, d=d: (bi, jnp.maximum((j * un) // d - 1, 0),
                                                    (j * un % d) // un))
        in_specs += [cur, prev, cur, prev, cur]
    args = []
    for bi in range(len(DILATIONS)):
        q, k, v = qkv[3 * bi:3 * bi + 3]
        args += [q, k, k, v, v]
    nbr = len(BRANCHES)
    return pl.pallas_call(
        _attn_kernel,
        grid=(b, steps),
        in_specs=in_specs,
        out_specs=pl.BlockSpec((1, ATTN_GROUP, ATTN_WIDTH), lambda bi, i: (bi, i // group_steps, 0)),
        out_shape=jax.ShapeDtypeStruct((b, s, ATTN_WIDTH), BF16),
        scratch_shapes=[pltpu.VMEM((ATTN_WIDTH // LANES, ATTN_GROUP, LANES), F32)] * (3 * nbr),
        compiler_params=pltpu.CompilerParams(
            dimension_semantics=("arbitrary", "arbitrary"), vmem_limit_bytes=VMEM_LIMIT),
        name="dilated_attention",
    )(*args)


def _tail_kernel(x_ref, attn_ref, y_ref, z_ref, p_ref, gssd_ref, gmix_ref, wout_ref, gpre_ref,
                 wup_ref, wdown_ref, gmlp_ref, wgate_ref, wproj_ref, gple_ref, out_ref):
    def gate_ssd(d):
        zf = z_ref[d["rows"], :].astype(F32)
        y = y_ref[d["rows"], :].astype(F32) * (zf * _sigmoid(zf))
        d["y"] = (_rms(y) * gssd_ref[...]).astype(BF16)

    def out_proj(d):
        d["mix"] = (_dot(attn_ref[d["rows"], :], wout_ref[0:ATTN_WIDTH, :])
                    + _dot(d["y"], wout_ref[ATTN_WIDTH:, :]))

    def norm_mix(d):
        d["h"] = x_ref[d["rows"], :] + _rms(d["mix"]) * gmix_ref[...]
        d["u"] = (_rms(d["h"]) * gpre_ref[...]).astype(BF16)

    def mlp(d):
        ff = jnp.zeros(d["h"].shape, F32)
        for c in range(D_FF // FF_CHUNK):
            sl = slice(c * FF_CHUNK, (c + 1) * FF_CHUNK)
            hid = jnp.square(jnp.maximum(_dot(d["u"], wup_ref[:, sl]), 0.0)).astype(BF16)
            ff = ff + _dot(hid, wdown_ref[sl, :])
        d["ff"] = ff

    def norm_mlp(d):
        d["h"] = d["h"] + _rms(d["ff"]) * gmlp_ref[...]

    def ple_proj(d):
        d["gate"] = _dot(d["h"].astype(BF16), wgate_ref[...])
        d["ple"] = _dot(p_ref[d["rows"], :].astype(BF16), wproj_ref[...])

    def norm_ple(d):
        ple = d["ple"] * _sigmoid(d["gate"])
        out_ref[d["rows"], :] = d["h"] + _rms(ple) * gple_ref[...]

    half = x_ref.shape[0] // 2
    a = dict(rows=slice(0, half))
    b = dict(rows=slice(half, 2 * half))
    for stage, d in ((gate_ssd, a), (out_proj, a), (gate_ssd, b), (norm_mix, a), (out_proj, b),
                     (mlp, a), (norm_mix, b), (mlp, b), (norm_mlp, a), (ple_proj, a),
                     (norm_mlp, b), (norm_ple, a), (ple_proj, b), (norm_ple, b)):
        stage(d)


def _tail(x2, attn2, y2, z2, p2, gssd, gmix, wout, gpre, wup, wdown, gmlp, wgate, wproj, gple):
    n = x2.shape[0]
    t = TAIL_TILE
    tile = lambda w_: pl.BlockSpec((t, w_), lambda i: (i, 0))
    full = lambda a: pl.BlockSpec(a.shape, lambda i: (0, 0), pipeline_mode=pl.Buffered(1))
    return pl.pallas_call(
        _tail_kernel,
        grid=(n // t,),
        in_specs=[tile(D_MODEL), tile(ATTN_WIDTH), tile(SSD_WIDTH), tile(SSD_WIDTH),
                  tile(PLE_DIM), full(gssd), full(gmix),
                  full(wout), full(gpre), full(wup), full(wdown), full(gmlp), full(wgate),
                  full(wproj), full(gple)],
        out_specs=tile(D_MODEL),
        out_shape=jax.ShapeDtypeStruct((n, D_MODEL), F32),
        compiler_params=pltpu.CompilerParams(
            dimension_semantics=("arbitrary",), vmem_limit_bytes=VMEM_LIMIT),
        name="tail",
    )(x2, attn2, y2, z2, p2, gssd, gmix, wout, gpre, wup, wdown, gmlp, wgate, wproj, gple)


def _pad_lanes(a):
    return jnp.pad(a, ((0, 0), (0, LANES - a.shape[-1])))


def kernel(x, p, positions, norm_mix_pre, norm_mix_post, w_in, conv_w, conv_b, dt_bias, a_log,
           d_skip, ssd_norm_g, w_out, norm_mlp_pre, norm_mlp_post, w_up, w_down, w_ple_gate,
           w_ple_proj, norm_ple_post):
    b, s, _ = x.shape
    half = HEAD_DIM // 2
    invf = (ROPE_THETA ** (-jnp.arange(half, dtype=F32) * 2.0 / HEAD_DIM)).reshape(half, 1)
    pos3 = positions.reshape(b, 1, s)
    row = lambda a: a.reshape(1, -1)

    h = x
    for i in range(w_in.shape[0]):
        *qkv, z, y, wout, wup, wdown, wgate, wproj = _in_proj(
            h, pos3, invf, row(norm_mix_pre[i]), w_in[i],
            conv_w[i], row(conv_b[i]), _pad_lanes(row(dt_bias[i])), _pad_lanes(row(a_log[i])),
            jnp.pad(a_log[i], (0, SSD_HEADS)).reshape(2 * SSD_HEADS, 1),
            row(jnp.repeat(d_skip[i], SSD_HEAD_DIM)),
            [w_out[i], w_up[i], w_down[i], w_ple_gate[i], w_ple_proj[i]])
        attn = _attention(qkv)
        h = _tail(h.reshape(b * s, D_MODEL), attn.reshape(b * s, ATTN_WIDTH),
                  y.reshape(b * s, SSD_WIDTH), z.reshape(b * s, SSD_WIDTH),
                  p[i].reshape(b * s, PLE_DIM), row(ssd_norm_g[i]),
                  row(norm_mix_post[i]), wout, row(norm_mlp_pre[i]), wup, wdown,
                  row(norm_mlp_post[i]), wgate, wproj,
                  row(norm_ple_post[i])).reshape(b, s, D_MODEL)
    return h
```

```python
import functools
import math

import jax
import jax.numpy as jnp
from jax import lax
from jax.experimental import pallas as pl
from jax.experimental.pallas import tpu as pltpu

D_MODEL = 1024
PLE_DIM = 256
N_HEADS = 8
HEAD_DIM = 64
ATTN_WIDTH = N_HEADS * HEAD_DIM
SSD_HEADS = 8
SSD_HEAD_DIM = 64
SSD_WIDTH = SSD_HEADS * SSD_HEAD_DIM
SSD_STATE = 128
CONV_WIDTH = 4
CONV_CH = SSD_WIDTH + 2 * SSD_STATE
CHUNK = 128
D_FF = 4 * D_MODEL
ROPE_THETA = 10000.0
BRANCHES = ((128, 1), (512, 4), (2048, 16))
DILATIONS = tuple(d for _, d in BRANCHES)
ATTN_BLOCK = 128
NORM_EPS = 1e-6
Q_SCALE = HEAD_DIM ** -0.5 * math.log2(math.e)

LANES = 128
BF16_ROWS = 16
TAIL_WEIGHT_ROWS = (ATTN_WIDTH + SSD_WIDTH, D_MODEL, D_FF, D_MODEL, PLE_DIM)
CONV_HALO = 8
PROJ_TILE = 512
TAIL_TILE = 512
FF_CHUNK = 1024
MERGE_ROWS = 256
ATTN_UNROLL = 4
ATTN_GROUP = ATTN_BLOCK * max(DILATIONS)
VMEM_LIMIT = 56 * 1024 * 1024

F32 = jnp.float32
BF16 = jnp.bfloat16


def _dot(a, b):
    return jnp.dot(a, b, preferred_element_type=F32)


def _dot_nt(a, b):
    return lax.dot_general(a, b, (((1,), (1,)), ((), ())), preferred_element_type=F32)


def _sigmoid(x):
    return 1.0 / (1.0 + jnp.exp(-x))


def _rms(x):
    return x * lax.rsqrt(jnp.mean(x * x, axis=-1, keepdims=True) + NORM_EPS)


def _split3(a):
    hi = a.astype(BF16)
    rest = a - hi.astype(F32)
    mid = rest.astype(BF16)
    return hi, mid, (rest - mid.astype(F32)).astype(BF16)


def _ssd_prepare(bm, cm, dt, a_row, a_col, tril):
    cs = sum(_dot(tril, piece) for piece in _split3(dt * a_row))
    dt_t = dt.T[0:2 * SSD_HEADS]
    cs_t = sum(_dot_nt(piece, tril) for piece in _split3(dt_t * a_col))
    w_t = dt_t * jnp.exp2(cs_t[:, CHUNK - 1:CHUNK] - cs_t)
    return dict(cs=cs, cs_dt_t=cs_t - jnp.log2(dt_t), w_t=w_t, cb=_dot_nt(cm, bm),
                cm_f=cm.astype(F32), bm_t=bm.astype(F32).T)


def _ssd_chunk(xs_b, pre, dskip, prev, causal, low):
    zero = jnp.zeros((CHUNK, LANES), BF16)
    ys, nxt = [], []
    for g in range(SSD_WIDTH // LANES):
        x2 = xs_b[:, g * LANES:(g + 1) * LANES]
        rhs = jnp.concatenate([x2, prev[g].astype(BF16)], axis=0)
        y_lhs, inj_lhs, dec2 = [], [], []
        for h in (2 * g, 2 * g + 1):
            cs_h = jnp.broadcast_to(pre["cs"][:, h:h + 1], (CHUNK, CHUNK))
            grow = jnp.exp2(cs_h)
            seg = jnp.exp2(jnp.where(causal, cs_h - pre["cs_dt_t"][h:h + 1, :], -jnp.inf))
            y_lhs.append(jnp.concatenate([pre["cb"] * seg, pre["cm_f"] * grow],
                                         axis=1).astype(BF16))
            inj_lhs.append((pre["bm_t"] * pre["w_t"][h:h + 1, :]).astype(BF16))
            dec2.append(grow[CHUNK - 1:CHUNK, :])
        y_pair = _dot(jnp.concatenate(y_lhs, axis=0), rhs)
        ys.append(jnp.where(low, y_pair[:CHUNK], y_pair[CHUNK:]))
        x_split = jnp.concatenate([jnp.where(low, x2, zero), jnp.where(low, zero, x2)], axis=0)
        inject = _dot(jnp.concatenate(inj_lhs, axis=1), x_split)
        nxt.append(prev[g] * jnp.where(low[0:1], dec2[0], dec2[1]) + inject)

    return (jnp.concatenate(ys, axis=1) + dskip * xs_b.astype(F32)).astype(BF16), nxt


def _in_proj_kernel(x_ref, pos_ref, invf_ref, g_ref, win_ref, convw_ref, convb_ref, dtb_ref,
                    alog_ref, alogc_ref, dskip_ref, *rest):
    n_cast = len(TAIL_WEIGHT_ROWS)
    cast_src, rest = rest[:n_cast], rest[n_cast:]
    widths = (ATTN_WIDTH, ATTN_WIDTH, ATTN_WIDTH, SSD_WIDTH, CONV_CH, SSD_HEADS)
    edges = [sum(widths[:c]) for c in range(len(widths) + 1)]
    wq_ref, wk_ref, wv_ref, wz_ref, wxbc_ref, wdt_ref = (
        win_ref.at[:, lo:hi] for lo, hi in zip(edges, edges[1:]))
    n_qkv = 3 * len(DILATIONS)
    qkv_refs = rest[:n_qkv]
    z_ref, y_ref = rest[n_qkv:n_qkv + 2]
    cast_dst = rest[n_qkv + 2:n_qkv + 2 + n_cast]
    cbuf, qs, ks, vs = rest[n_qkv + 2 + n_cast:n_qkv + 6 + n_cast]
    stages, state = rest[n_qkv + 6 + n_cast:-1], rest[-1]
    t = x_ref.shape[1]

    @pl.when(pl.program_id(1) == 0)
    def _():
        cbuf[0:CONV_HALO, :] = jnp.zeros((CONV_HALO, CONV_CH), F32)
        state[...] = jnp.zeros(state.shape, F32)

    u = (_rms(x_ref[0]) * g_ref[...]).astype(BF16)

    ang = invf_ref[...] * pos_ref[0].astype(F32)
    cos4 = jnp.concatenate([jnp.cos(ang)] * 4, axis=0).T
    sin4 = jnp.concatenate([jnp.sin(ang)] * 4, axis=0).T
    lane = lax.broadcasted_iota(jnp.int32, (t, LANES), 1)
    first_half = (lane % HEAD_DIM) < (HEAD_DIM // 2)
    sin_signed = jnp.where(first_half, -sin4, sin4)

    def rope(v):
        partner = jnp.where(first_half, pltpu.roll(v, LANES - HEAD_DIM // 2, 1),
                            pltpu.roll(v, HEAD_DIM // 2, 1))
        return v * cos4 + partner * sin_signed

    def emit_layouts(ti):
        planes, d_prev = (qs, ks, vs), 1
        for bi, d in enumerate(DILATIONS):
            f = d // d_prev
            staged = stages[3 * (bi - 1):3 * bi] if 0 < bi < len(DILATIONS) - 1 else (None,) * 3
            src, stage, dst = planes[ti], staged[ti], qkv_refs[3 * bi + ti]
            for r in range(d):
                start = (r % d_prev) * (t // d_prev) + r // d_prev
                for g in range(ATTN_WIDTH // LANES):
                    val = src[g] if d == 1 else src[g, pl.ds(start, t // d, stride=f), :]
                    if stage is not None:
                        stage[g, r * (t // d):(r + 1) * (t // d), :] = val
                    c = r * ATTN_WIDTH + g * LANES
                    dst[0, :, c:c + LANES] = val.astype(BF16)
            if staged[0] is not None:
                planes, d_prev = staged, d

    lanes = [slice(g * LANES, (g + 1) * LANES) for g in range(ATTN_WIDTH // LANES)]
    xbc = _dot(u, wxbc_ref[...])
    dtr = jnp.pad(_dot(u, wdt_ref[...]), ((0, 0), (0, LANES - SSD_HEADS))) + dtb_ref[...]
    q = _dot(u, wq_ref[...])

    cbuf[CONV_HALO:CONV_HALO + t, :] = xbc
    w = convw_ref[...]
    y = convb_ref[...] + w[CONV_WIDTH - 1:CONV_WIDTH] * xbc
    for s in range(1, CONV_WIDTH):
        y = y + w[CONV_WIDTH - 1 - s:CONV_WIDTH - s] * cbuf[CONV_HALO - s:CONV_HALO - s + t, :]
    cbuf[0:CONV_HALO, :] = cbuf[t:t + CONV_HALO, :]
    act = y * _sigmoid(y)
    xs_b = act[:, :SSD_WIDTH].astype(BF16)
    bm = act[:, SSD_WIDTH:SSD_WIDTH + SSD_STATE].astype(BF16)
    cm = act[:, SSD_WIDTH + SSD_STATE:].astype(BF16)

    dt = jnp.maximum(dtr, 0.0) + jnp.log1p(jnp.exp(-jnp.abs(dtr)))

    k = _dot(u, wk_ref[...])
    for g, sl in enumerate(lanes):
        qs[g] = rope(q[:, sl]) * Q_SCALE
    v = _dot(u, wv_ref[...])
    for g, sl in enumerate(lanes):
        ks[g] = rope(k[:, sl])

    row = lax.broadcasted_iota(jnp.int32, (CHUNK, CHUNK), 0)
    col = lax.broadcasted_iota(jnp.int32, (CHUNK, CHUNK), 1)
    causal = row >= col
    low = col < SSD_HEAD_DIM
    tril = causal.astype(BF16)
    a_row = -jnp.exp(alog_ref[...]) * math.log2(math.e)
    a_col = -jnp.exp(alogc_ref[...]) * math.log2(math.e)
    groups = SSD_WIDTH // LANES
    st = [state[:, g * LANES:(g + 1) * LANES] for g in range(groups)]

    chunk_rows = [slice(c * CHUNK, (c + 1) * CHUNK) for c in range(t // CHUNK)]
    pres = [_ssd_prepare(bm[rows], cm[rows], dt[rows], a_row, a_col, tril) for rows in chunk_rows]

    def ssd_step(c, st):
        rows = chunk_rows[c]
        yc, st = _ssd_chunk(xs_b[rows], pres[c], dskip_ref[...], st, causal, low)
        y_ref[0, rows, :] = yc
        return st

    assert t // CHUNK == 4
    st = ssd_step(0, st)
    z_ref[0] = _dot(u, wz_ref[...]).astype(BF16)
    emit_layouts(0)
    st = ssd_step(1, st)
    emit_layouts(1)
    st = ssd_step(2, st)
    for g, sl in enumerate(lanes):
        vs[g] = v[:, sl]
    emit_layouts(2)
    st = ssd_step(3, st)
    for g in range(groups):
        state[:, g * LANES:(g + 1) * LANES] = st[g]

    for src, dst in zip(cast_src, cast_dst):
        dst[...] = src[...].astype(BF16)


def _in_proj(x, pos3, invf, g, win, convw, convb, dtb, alog, alogc, dskip, tail_weights):
    b, s, _ = x.shape
    t = PROJ_TILE
    tile = lambda w_: pl.BlockSpec((1, t, w_), lambda bi, j: (bi, j, 0))
    full = lambda a: pl.BlockSpec(a.shape, lambda bi, j: (0,) * a.ndim)
    wide = jax.ShapeDtypeStruct((b, s, ATTN_WIDTH), BF16)
    qkv_specs, qkv_shapes = [], []
    for d in DILATIONS:
        qkv_specs += [pl.BlockSpec((1, t // d, d * ATTN_WIDTH), lambda bi, j: (bi, j, 0))] * 3
        qkv_shapes += [jax.ShapeDtypeStruct((b, s // d, d * ATTN_WIDTH), BF16)] * 3
    plane = pltpu.VMEM((ATTN_WIDTH // LANES, t, LANES), F32)
    steps = b * (s // t)
    cast_specs = []
    for w_ in tail_weights:
        assert w_.shape[0] in TAIL_WEIGHT_ROWS and w_.shape[0] % steps == 0
        per_step = w_.shape[0] // steps
        rows = max(per_step, BF16_ROWS)
        cast_specs.append(pl.BlockSpec(
            (rows, w_.shape[1]),
            lambda bi, j, per_step=per_step, rows=rows: ((bi * (s // t) + j) * per_step // rows, 0)))
    return pl.pallas_call(
        _in_proj_kernel,
        grid=(b, s // t),
        in_specs=[tile(D_MODEL), pl.BlockSpec((1, 1, t), lambda bi, j: (bi, 0, j)),
                  full(invf), full(g), full(win), full(convw), full(convb), full(dtb), full(alog), full(alogc),
                  full(dskip)] + cast_specs,
        out_specs=qkv_specs + [tile(ATTN_WIDTH), tile(SSD_WIDTH)] + cast_specs,
        out_shape=qkv_shapes + [wide, wide]
                  + [jax.ShapeDtypeStruct(w_.shape, BF16) for w_ in tail_weights],
        scratch_shapes=[pltpu.VMEM((t + CONV_HALO, CONV_CH), F32)]
                       + [plane] * (3 * (len(DILATIONS) - 1))
                       + [pltpu.VMEM((SSD_STATE, SSD_WIDTH), F32)],
        compiler_params=pltpu.CompilerParams(
            dimension_semantics=("arbitrary", "arbitrary"), vmem_limit_bytes=VMEM_LIMIT),
        name="in_proj",
    )(x, pos3, invf, g, win, convw, convb, dtb, alog, alogc, dskip, *tail_weights)


def _attn_block(has_prev, span, q, kp, kc, vp, vc):
    blk = ATTN_BLOCK
    qi = lax.broadcasted_iota(jnp.int32, (blk, 2 * blk), 0)
    ki = lax.broadcasted_iota(jnp.int32, (blk, 2 * blk), 1)
    dist = qi + blk - ki
    valid = (dist >= 0) & (dist <= span) & ((ki >= blk) | has_prev)
    bias = jnp.where(valid, 0.0, -jnp.inf).astype(F32)
    bias = jnp.concatenate([bias, bias], axis=0)
    low = lax.broadcasted_iota(jnp.int32, (blk, LANES), 1) < HEAD_DIM

    outs, maxes, sums = [], [], []
    for g in range(ATTN_WIDTH // LANES):
        sl = slice(g * LANES, (g + 1) * LANES)
        q2 = q(sl)
        k2 = jnp.concatenate([kp(sl), kc(sl)], axis=0)
        v2 = jnp.concatenate([vp(sl), vc(sl)], axis=0)
        zero = jnp.zeros_like(q2)
        qq = jnp.concatenate([jnp.where(low, q2, zero), jnp.where(low, zero, q2)], axis=0)
        s = _dot_nt(qq, k2) + bias
        m = jnp.max(s, axis=-1, keepdims=True)
        p = jnp.exp2(s - m)
        l = jnp.sum(p, axis=-1, keepdims=True)
        o = _dot(p.astype(BF16), v2)
        outs.append(jnp.where(low, o[:blk], o[blk:]))
        maxes.append(jnp.where(low, m[:blk], m[blk:]))
        sums.append(jnp.where(low, l[:blk], l[blk:]))
    return outs, maxes, sums


def _attn_kernel(*refs):
    nbr = len(BRANCHES)
    in_refs = refs[:5 * nbr]
    out_ref = refs[5 * nbr]
    acc_o = refs[5 * nbr + 1:5 * nbr + 1 + nbr]
    acc_m = refs[5 * nbr + 1 + nbr:5 * nbr + 1 + 2 * nbr]
    acc_l = refs[5 * nbr + 1 + 2 * nbr:]
    group_steps = ATTN_GROUP // ATTN_BLOCK
    blk, wid = ATTN_BLOCK, ATTN_WIDTH

    for sub in range(ATTN_UNROLL):
        i = pl.program_id(1) * ATTN_UNROLL + sub
        local = i % group_steps
        for bi, (window, d) in enumerate(BRANCHES):
            q_ref, kp_ref, kc_ref, vp_ref, vc_ref = in_refs[5 * bi:5 * bi + 5]
            if d == 1:
                rows = slice(sub * blk, (sub + 1) * blk)
                before = slice((sub - 1) * blk, sub * blk)
                prev_k = (kp_ref, slice(0, blk)) if sub == 0 else (kc_ref, before)
                prev_v = (vp_ref, slice(0, blk)) if sub == 0 else (vc_ref, before)
                srcs = ((q_ref, rows), prev_k, (kc_ref, rows), prev_v, (vc_ref, rows))
                ops = tuple((lambda sl, ref=ref, rr=rr: ref[0, rr, sl]) for ref, rr in srcs)
                has_prev = (i > 0) if sub == 0 else True
            else:
                base = sub * wid
                ops = tuple((lambda sl, ref=ref: ref[0, :, base + sl.start:base + sl.stop])
                            for ref in (q_ref, kp_ref, kc_ref, vp_ref, vc_ref))
                has_prev = (i // d) > 0
            o, m, l = _attn_block(has_prev, window // d, *ops)
            start = (local // d) * (d * blk) + i % d
            rows = pl.ds(start, blk) if d == 1 else pl.ds(start, blk, stride=d)
            for g in range(wid // LANES):
                acc_o[bi][g, rows, :] = o[g]
                acc_m[bi][g, rows, :] = m[g]
                acc_l[bi][g, rows, :] = l[g]

    @pl.when(local == group_steps - 1)
    def _():
        total = lambda xs: functools.reduce(lambda a, b_: a + b_, xs)

        def merge(c, carry):
            rows = pl.ds(pl.multiple_of(c * MERGE_ROWS, MERGE_ROWS), MERGE_ROWS)
            for g in range(ATTN_WIDTH // LANES):
                ms = [a[g, rows, :] for a in acc_m]
                top = functools.reduce(jnp.maximum, ms)
                es = [jnp.exp2(m - top) for m in ms]
                den = total([e * a[g, rows, :] for e, a in zip(es, acc_l)])
                num = total([e * a[g, rows, :] for e, a in zip(es, acc_o)])
                out_ref[0, rows, g * LANES:(g + 1) * LANES] = (num / den).astype(BF16)
            return carry

        lax.fori_loop(0, ATTN_GROUP // MERGE_ROWS, merge, 0)


def _attention(qkv):
    b = qkv[0].shape[0]
    s = qkv[0].shape[1] * DILATIONS[0]
    un = ATTN_UNROLL
    steps = s // (ATTN_BLOCK * un)
    group_steps = ATTN_GROUP // (ATTN_BLOCK * un)
    in_specs = []
    for d in DILATIONS:
        if d == 1:
            cur = pl.BlockSpec((1, un * ATTN_BLOCK, ATTN_WIDTH), lambda bi, j: (bi, j, 0))
            prev = pl.BlockSpec((1, ATTN_BLOCK, ATTN_WIDTH),
                                lambda bi, j: (bi, jnp.maximum(j * un - 1, 0), 0))
        else:
            assert d % un == 0
            cur = pl.BlockSpec((1, ATTN_BLOCK, un * ATTN_WIDTH),
                               lambda bi, j, d=d: (bi, (j * un) // d, (j * un % d) // un))
            prev = pl.BlockSpec((1, ATTN_BLOCK, un * ATTN_WIDTH),
                                lambda bi, j, d=d: (bi, jnp.maximum((j * un) // d - 1, 0),
                                                    (j * un % d) // un))
        in_specs += [cur, prev, cur, prev, cur]
    args = []
    for bi in range(len(DILATIONS)):
        q, k, v = qkv[3 * bi:3 * bi + 3]
        args += [q, k, k, v, v]
    nbr = len(BRANCHES)
    return pl.pallas_call(
        _attn_kernel,
        grid=(b, steps),
        in_specs=in_specs,
        out_specs=pl.BlockSpec((1, ATTN_GROUP, ATTN_WIDTH), lambda bi, i: (bi, i // group_steps, 0)),
        out_shape=jax.ShapeDtypeStruct((b, s, ATTN_WIDTH), BF16),
        scratch_shapes=[pltpu.VMEM((ATTN_WIDTH // LANES, ATTN_GROUP, LANES), F32)] * (3 * nbr),
        compiler_params=pltpu.CompilerParams(
            dimension_semantics=("arbitrary", "arbitrary"), vmem_limit_bytes=VMEM_LIMIT),
        name="dilated_attention",
    )(*args)


def _tail_kernel(x_ref, attn_ref, y_ref, z_ref, p_ref, gssd_ref, gmix_ref, wout_ref, gpre_ref,
                 wup_ref, wdown_ref, gmlp_ref, wgate_ref, wproj_ref, gple_ref, out_ref):
    def gate_ssd(d):
        zf = z_ref[d["rows"], :].astype(F32)
        y = y_ref[d["rows"], :].astype(F32) * (zf * _sigmoid(zf))
        d["y"] = (_rms(y) * gssd_ref[...]).astype(BF16)

    def out_proj(d):
        d["mix"] = (_dot(attn_ref[d["rows"], :], wout_ref[0:ATTN_WIDTH, :])
                    + _dot(d["y"], wout_ref[ATTN_WIDTH:, :]))

    def norm_mix(d):
        d["h"] = x_ref[d["rows"], :] + _rms(d["mix"]) * gmix_ref[...]
        d["u"] = (_rms(d["h"]) * gpre_ref[...]).astype(BF16)

    def mlp(d):
        ff = jnp.zeros(d["h"].shape, F32)
        for c in range(D_FF // FF_CHUNK):
            sl = slice(c * FF_CHUNK, (c + 1) * FF_CHUNK)
            hid = jnp.square(jnp.maximum(_dot(d["u"], wup_ref[:, sl]), 0.0)).astype(BF16)
            ff = ff + _dot(hid, wdown_ref[sl, :])
        d["ff"] = ff

    def norm_mlp(d):
        d["h"] = d["h"] + _rms(d["ff"]) * gmlp_ref[...]

    def ple_proj(d):
        d["gate"] = _dot(d["h"].astype(BF16), wgate_ref[...])
        d["ple"] = _dot(p_ref[d["rows"], :].astype(BF16), wproj_ref[...])

    def norm_ple(d):
        ple = d["ple"] * _sigmoid(d["gate"])
        out_ref[d["rows"], :] = d["h"] + _rms(ple) * gple_ref[...]

    half = x_ref.shape[0] // 2
    a = dict(rows=slice(0, half))
    b = dict(rows=slice(half, 2 * half))
    for stage, d in ((gate_ssd, a), (out_proj, a), (gate_ssd, b), (norm_mix, a), (out_proj, b),
                     (mlp, a), (norm_mix, b), (mlp, b), (norm_mlp, a), (ple_proj, a),
                     (norm_mlp, b), (norm_ple, a), (ple_proj, b), (norm_ple, b)):
        stage(d)


def _tail(x2, attn2, y2, z2, p2, gssd, gmix, wout, gpre, wup, wdown, gmlp, wgate, wproj, gple):
    n = x2.shape[0]
    t = TAIL_TILE
    tile = lambda w_: pl.BlockSpec((t, w_), lambda i: (i, 0))
    full = lambda a: pl.BlockSpec(a.shape, lambda i: (0, 0), pipeline_mode=pl.Buffered(1))
    return pl.pallas_call(
        _tail_kernel,
        grid=(n // t,),
        in_specs=[tile(D_MODEL), tile(ATTN_WIDTH), tile(SSD_WIDTH), tile(SSD_WIDTH),
                  tile(PLE_DIM), full(gssd), full(gmix),
                  full(wout), full(gpre), full(wup), full(wdown), full(gmlp), full(wgate),
                  full(wproj), full(gple)],
        out_specs=tile(D_MODEL),
        out_shape=jax.ShapeDtypeStruct((n, D_MODEL), F32),
        compiler_params=pltpu.CompilerParams(
            dimension_semantics=("arbitrary",), vmem_limit_bytes=VMEM_LIMIT),
        name="tail",
    )(x2, attn2, y2, z2, p2, gssd, gmix, wout, gpre, wup, wdown, gmlp, wgate, wproj, gple)


def _pad_lanes(a):
    return jnp.pad(a, ((0, 0), (0, LANES - a.shape[-1])))


def kernel(x, p, positions, norm_mix_pre, norm_mix_post, w_in, conv_w, conv_b, dt_bias, a_log,
           d_skip, ssd_norm_g, w_out, norm_mlp_pre, norm_mlp_post, w_up, w_down, w_ple_gate,
           w_ple_proj, norm_ple_post):
    b, s, _ = x.shape
    half = HEAD_DIM // 2
    invf = (ROPE_THETA ** (-jnp.arange(half, dtype=F32) * 2.0 / HEAD_DIM)).reshape(half, 1)
    pos3 = positions.reshape(b, 1, s)
    row = lambda a: a.reshape(1, -1)

    h = x
    for i in range(w_in.shape[0]):
        *qkv, z, y, wout, wup, wdown, wgate, wproj = _in_proj(
            h, pos3, invf, row(norm_mix_pre[i]), w_in[i].astype(BF16),
            conv_w[i], row(conv_b[i]), _pad_lanes(row(dt_bias[i])), _pad_lanes(row(a_log[i])),
            jnp.pad(a_log[i], (0, SSD_HEADS)).reshape(2 * SSD_HEADS, 1),
            row(jnp.repeat(d_skip[i], SSD_HEAD_DIM)),
            [w_out[i], w_up[i], w_down[i], w_ple_gate[i], w_ple_proj[i]])
        attn = _attention(qkv)
        h = _tail(h.reshape(b * s, D_MODEL), attn.reshape(b * s, ATTN_WIDTH),
                  y.reshape(b * s, SSD_WIDTH), z.reshape(b * s, SSD_WIDTH),
                  p[i].reshape(b * s, PLE_DIM), row(ssd_norm_g[i]),
                  row(norm_mix_post[i]), wout, row(norm_mlp_pre[i]), wup, wdown,
                  row(norm_mlp_post[i]), wgate, wproj,
                  row(norm_ple_post[i])).reshape(b, s, D_MODEL)
    return h
```
